```python
import math
import jax, jax.numpy as jnp
from jax import lax
import numpy as np

D_MODEL = 2048
BATCH = 4
SEQ = 4096
DEPTH = 2

CHUNK = 64
N_MIXERS = 2
EPS = 1e-6
S5_GROUP_CH = 16
S5_GROUPS = D_MODEL // S5_GROUP_CH
S5_STATE = 64
S5_DT_MIN = 1e-3
S5_DT_MAX = 1e-1
ML_HEADS = 8
ML_QK = D_MODEL // 2
ML_V = D_MODEL
ML_DQK = ML_QK // ML_HEADS
ML_DV = ML_V // ML_HEADS
ML_IN = 2 * ML_QK + 2 * ML_V + 2 * ML_HEADS
GATE_SOFTCAP = 15.0
MOE_GROUPS = 8
MOE_PER_GROUP = 8
MOE_EXPERTS = MOE_GROUPS * MOE_PER_GROUP
MOE_TOPK = 2
MOE_HIDDEN = 768
MOE_BLOCK = 128

kernel_name = "hybrid_s5_mlstm_hmoe_trunk"


def rmsnorm(x, g):
    x32 = x.astype(jnp.float32)
    r = x32 * lax.rsqrt(jnp.mean(x32 * x32, axis=-1, keepdims=True) + EPS)
    return (r * g.astype(jnp.float32)).astype(x.dtype)


def s5_mixer(h, lam_re, lam_im, log_dt, b_re, b_im, c_re, c_im, d_skip, w_glu):
    bsz, s, d = h.shape
    h32 = h.astype(jnp.float32)
    u = h32.reshape(bsz, s, S5_GROUPS, S5_GROUP_CH)
    lam = lax.complex(lam_re.astype(jnp.float32), lam_im.astype(jnp.float32))
    dt = jnp.exp(log_dt.astype(jnp.float32))[:, None]
    lam_bar = jnp.exp(lam * dt)
    b_c = lax.complex(b_re.astype(jnp.float32), b_im.astype(jnp.float32))
    b_bar = ((lam_bar - 1.0) / lam)[..., None] * b_c
    bu = lax.complex(jnp.einsum('bsgc,gpc->bsgp', u, jnp.real(b_bar)),
                     jnp.einsum('bsgc,gpc->bsgp', u, jnp.imag(b_bar)))
    a = jnp.broadcast_to(lam_bar[None, None], (1, s, S5_GROUPS, S5_STATE))

    def combine(e1, e2):
        a1, b1 = e1
        a2, b2 = e2
        return a1 * a2, a2 * b1 + b2

    _, states = lax.associative_scan(combine, (a, bu), axis=1)
    y = (jnp.einsum('bsgp,gcp->bsgc', jnp.real(states), c_re.astype(jnp.float32))
         - jnp.einsum('bsgp,gcp->bsgc', jnp.imag(states), c_im.astype(jnp.float32)))
    y = y.reshape(bsz, s, d) + d_skip.astype(jnp.float32) * h32
    g = jax.nn.gelu(y).astype(h.dtype)
    vg = g @ w_glu
    val, gate = vg[..., :d], vg[..., d:]
    return (val * jax.nn.sigmoid(gate)).astype(h.dtype)


def mlstm_chunkwise(q, k, v, ig, lf):
    bsz, nh, s, dk = q.shape
    dv = v.shape[-1]
    nc = s // CHUNK

    def to_chunks(t):
        return jnp.moveaxis(t.reshape(bsz, nh, nc, CHUNK, *t.shape[3:]), 2, 0)

    qc, kc, vc, ic, fc = (to_chunks(t) for t in (q, k, v, ig, lf))
    causal = jnp.tril(jnp.ones((CHUNK, CHUNK), dtype=bool))

    def step(carry, xs):
        c_mat, n_vec, m = carry
        q_, k_, v_, i_, f_ = xs
        bcum = jnp.cumsum(f_, axis=-1)
        a_log = bcum + m[..., None]
        d_log = bcum[..., :, None] - bcum[..., None, :] + i_[..., None, :]
        d_log = jnp.where(causal, d_log, -jnp.inf)
        m_t = jnp.maximum(a_log, jnp.max(d_log, axis=-1))
        w_intra = jnp.exp(d_log - m_t[..., None])
        w_inter = jnp.exp(a_log - m_t)
        sc = jnp.einsum('bhtd,bhsd->bhts', q_, k_) * w_intra
        num = (w_inter[..., None] * jnp.einsum('bhtd,bhde->bhte', q_, c_mat)
               + jnp.einsum('bhts,bhse->bhte', sc, v_))
        den = w_inter * jnp.einsum('bhtd,bhd->bht', q_, n_vec) + jnp.sum(sc, axis=-1)
        h_out = num / jnp.maximum(jnp.abs(den), jnp.exp(-m_t))[..., None]
        b_last = bcum[..., -1]
        g_log = b_last[..., None] - bcum + i_
        m_new = jnp.maximum(b_last + m, jnp.max(g_log, axis=-1))
        decay = jnp.exp(b_last + m - m_new)
        wk = jnp.exp(g_log - m_new[..., None])
        c_new = decay[..., None, None] * c_mat + jnp.einsum('bhs,bhsd,bhse->bhde', wk, k_, v_)
        n_new = decay[..., None] * n_vec + jnp.einsum('bhs,bhsd->bhd', wk, k_)
        return (c_new, n_new, m_new), h_out

    init = (jnp.zeros((bsz, nh, dk, dv), jnp.float32),
            jnp.zeros((bsz, nh, dk), jnp.float32),
            jnp.zeros((bsz, nh), jnp.float32))
    _, hs = lax.scan(step, init, (qc, kc, vc, ic, fc))
    return jnp.moveaxis(hs, 0, 2).reshape(bsz, nh, s, dv)


def mlstm_mixer(h, w_in, b_gate, g_head, w_out):
    bsz, s, _ = h.shape
    z = h @ w_in
    o1, o2, o3, o4, o5 = ML_QK, 2 * ML_QK, 2 * ML_QK + ML_V, 2 * ML_QK + 2 * ML_V, 2 * ML_QK + 2 * ML_V + ML_HEADS
    q, k, v, o = z[..., :o1], z[..., o1:o2], z[..., o2:o3], z[..., o3:o4]
    b32 = b_gate.astype(jnp.float32)
    ig_pre = z[..., o4:o5].astype(jnp.float32) + b32[:ML_HEADS]
    fg_pre = z[..., o5:].astype(jnp.float32) + b32[ML_HEADS:]
    ig = GATE_SOFTCAP * jnp.tanh(ig_pre / GATE_SOFTCAP)
    lf = jax.nn.log_sigmoid(GATE_SOFTCAP * jnp.tanh(fg_pre / GATE_SOFTCAP))

    def heads(t, dh):
        return t.astype(jnp.float32).reshape(bsz, s, ML_HEADS, dh).transpose(0, 2, 1, 3)

    qh = heads(q, ML_DQK)
    kh = heads(k, ML_DQK) * (ML_DQK ** -0.5)
    vh = heads(v, ML_DV)
    hh = mlstm_chunkwise(qh, kh, vh, ig.transpose(0, 2, 1), lf.transpose(0, 2, 1))
    hh = hh * lax.rsqrt(jnp.mean(hh * hh, axis=-1, keepdims=True) + EPS)
    hh = hh * g_head.astype(jnp.float32).reshape(ML_HEADS, 1, ML_DV)
    hh = hh.transpose(0, 2, 1, 3).reshape(bsz, s, ML_V)
    out = (jax.nn.sigmoid(o.astype(jnp.float32)) * hh).astype(h.dtype)
    return out @ w_out


def hier_moe(h, layer, w_group, b_group, w_expert, b_expert, w_gu, w_down):
    bsz, s, d = h.shape
    n_tok = bsz * s
    ht = h.reshape(n_tok, d)
    h32 = ht.astype(jnp.float32)
    lg = h32 @ w_group.astype(jnp.float32) + b_group.astype(jnp.float32)
    pg = jax.nn.softmax(lg, axis=-1)
    g_sel = jnp.argmax(lg, axis=-1).astype(jnp.int32)
    p_sel = jnp.take_along_axis(pg, g_sel[:, None], axis=1)
    le = (h32 @ w_expert.astype(jnp.float32) + b_expert.astype(jnp.float32)).reshape(n_tok, MOE_GROUPS, MOE_PER_GROUP)
    le_sel = jnp.take_along_axis(le, g_sel[:, None, None], axis=1)[:, 0]
    top_v, top_j = lax.top_k(le_sel, MOE_TOPK)
    gates = p_sel * jax.nn.softmax(top_v, axis=-1)
    expert_idx = g_sel[:, None] * MOE_PER_GROUP + top_j.astype(jnp.int32)

    n_asg = n_tok * MOE_TOPK
    n_blocks = n_asg // MOE_BLOCK + MOE_EXPERTS
    flat_e = expert_idx.reshape(-1)
    flat_g = gates.reshape(-1)
    flat_tok = jnp.arange(n_asg, dtype=jnp.int32) // MOE_TOPK
    order = jnp.argsort(flat_e, stable=True)
    sorted_e = flat_e[order]
    counts = jnp.bincount(flat_e, length=MOE_EXPERTS).astype(jnp.int32)
    padded = ((counts + MOE_BLOCK - 1) // MOE_BLOCK) * MOE_BLOCK
    pad_end = jnp.cumsum(padded)
    pad_start = pad_end - padded
    seg_start = jnp.cumsum(counts) - counts
    dest = pad_start[sorted_e] + (jnp.arange(n_asg, dtype=jnp.int32) - seg_start[sorted_e])
    buf_tok = jnp.zeros((n_blocks * MOE_BLOCK,), jnp.int32).at[dest].set(flat_tok[order])
    buf_w = jnp.zeros((n_blocks * MOE_BLOCK,), jnp.float32).at[dest].set(flat_g[order])
    block_start = jnp.arange(n_blocks, dtype=jnp.int32) * MOE_BLOCK
    block_expert = jnp.minimum(jnp.searchsorted(pad_end, block_start, side='right'),
                               MOE_EXPERTS - 1).astype(jnp.int32)

    def expert_block(args):
        tok, e = args
        xb = ht[tok]
        gu = xb @ w_gu[layer, e]
        act = jax.nn.silu(gu[:, :MOE_HIDDEN]) * gu[:, MOE_HIDDEN:]
        return act @ w_down[layer, e]

    yb = lax.map(expert_block, (buf_tok.reshape(n_blocks, MOE_BLOCK), block_expert))
    out = jnp.zeros((n_tok, d), jnp.float32).at[buf_tok].add(
        yb.reshape(-1, d).astype(jnp.float32) * buf_w[:, None])
    return out.reshape(bsz, s, d).astype(h.dtype)


def setup_inputs(seed: int = 0) -> dict:
    key = jax.random.key(seed)
    ks = jax.random.split(key, 32)
    f32 = jnp.float32
    nrm = jax.random.normal
    n_a = (DEPTH + N_MIXERS - 1) // N_MIXERS
    n_b = DEPTH // N_MIXERS
    D, G, P, CG = D_MODEL, S5_GROUPS, S5_STATE, S5_GROUP_CH
    x = nrm(ks[0], (BATCH, SEQ, D), f32)
    n_idx = jnp.arange(P, dtype=f32)
    s5_lam_re = -0.5 + 0.01 * nrm(ks[1], (n_a, G, P), f32)
    s5_lam_im = math.pi * n_idx + 0.01 * nrm(ks[2], (n_a, G, P), f32)
    s5_log_dt = jax.random.uniform(ks[3], (n_a, G), f32, minval=math.log(S5_DT_MIN), maxval=math.log(S5_DT_MAX))
    s5_b_re = nrm(ks[4], (n_a, G, P, CG), f32) * (2 * CG) ** -0.5
    s5_b_im = nrm(ks[5], (n_a, G, P, CG), f32) * (2 * CG) ** -0.5
    s5_c_re = nrm(ks[6], (n_a, G, CG, P), f32) * (2 * P) ** -0.5
    s5_c_im = nrm(ks[7], (n_a, G, CG, P), f32) * (2 * P) ** -0.5
    s5_d = nrm(ks[8], (n_a, D), f32)
    s5_w_glu = nrm(ks[9], (n_a, D, 2 * D), f32) * D ** -0.5
    ml_w_in = nrm(ks[10], (n_b, D, ML_IN), f32) * D ** -0.5
    ig_b = 0.01 * nrm(ks[11], (n_b, ML_HEADS), f32)
    fg_b = jnp.linspace(3.0, 6.0, ML_HEADS, dtype=f32) + 0.01 * nrm(ks[12], (n_b, ML_HEADS), f32)
    ml_b_gate = jnp.concatenate([ig_b, fg_b], axis=-1)
    ml_g_head = 1.0 + 0.01 * nrm(ks[13], (n_b, ML_V), f32)
    ml_w_out = nrm(ks[14], (n_b, ML_V, D), f32) * ML_V ** -0.5
    norm_mix = 1.0 + 0.01 * nrm(ks[15], (DEPTH, D), f32)
    norm_ffn = 1.0 + 0.01 * nrm(ks[16], (DEPTH, D), f32)
    moe_w_group = nrm(ks[17], (DEPTH, D, MOE_GROUPS), f32) * D ** -0.5
    moe_b_group = 0.01 * nrm(ks[18], (DEPTH, MOE_GROUPS), f32)
    moe_w_expert = nrm(ks[19], (DEPTH, D, MOE_EXPERTS), f32) * D ** -0.5
    moe_b_expert = 0.01 * nrm(ks[20], (DEPTH, MOE_EXPERTS), f32)
    moe_w_gu = nrm(ks[21], (DEPTH, MOE_EXPERTS, D, 2 * MOE_HIDDEN), f32) * D ** -0.5
    moe_w_down = nrm(ks[22], (DEPTH, MOE_EXPERTS, MOE_HIDDEN, D), f32) * MOE_HIDDEN ** -0.5
    norm_final = 1.0 + 0.01 * nrm(ks[23], (D,), f32)
    return {"x": x, "s5_lam_re": s5_lam_re, "s5_lam_im": s5_lam_im, "s5_log_dt": s5_log_dt,
            "s5_b_re": s5_b_re, "s5_b_im": s5_b_im, "s5_c_re": s5_c_re, "s5_c_im": s5_c_im,
            "s5_d": s5_d, "s5_w_glu": s5_w_glu, "ml_w_in": ml_w_in, "ml_b_gate": ml_b_gate,
            "ml_g_head": ml_g_head, "ml_w_out": ml_w_out, "norm_mix": norm_mix, "norm_ffn": norm_ffn,
            "moe_w_group": moe_w_group, "moe_b_group": moe_b_group, "moe_w_expert": moe_w_expert,
            "moe_b_expert": moe_b_expert, "moe_w_gu": moe_w_gu, "moe_w_down": moe_w_down,
            "norm_final": norm_final}


def reference(x, s5_lam_re, s5_lam_im, s5_log_dt, s5_b_re, s5_b_im, s5_c_re, s5_c_im,
              s5_d, s5_w_glu, ml_w_in, ml_b_gate, ml_g_head, ml_w_out, norm_mix, norm_ffn,
              moe_w_group, moe_b_group, moe_w_expert, moe_b_expert, moe_w_gu, moe_w_down,
              norm_final):
    for layer in range(DEPTH):
        j = layer // N_MIXERS
        h = rmsnorm(x, norm_mix[layer])
        if layer % N_MIXERS == 0:
            mix = s5_mixer(h, s5_lam_re[j], s5_lam_im[j], s5_log_dt[j], s5_b_re[j], s5_b_im[j],
                           s5_c_re[j], s5_c_im[j], s5_d[j], s5_w_glu[j])
        else:
            mix = mlstm_mixer(h, ml_w_in[j], ml_b_gate[j], ml_g_head[j], ml_w_out[j])
        x = x + mix
        h = rmsnorm(x, norm_ffn[layer])
        x = x + hier_moe(h, layer, moe_w_group[layer], moe_b_group[layer], moe_w_expert[layer],
                         moe_b_expert[layer], moe_w_gu, moe_w_down)
    return rmsnorm(x, norm_final)
```

```python
import functools
import math

import jax
import jax.numpy as jnp
from jax import lax
from jax.experimental import pallas as pl
from jax.experimental.pallas import tpu as pltpu

F32 = jnp.float32
BF16 = jnp.bfloat16
I32 = jnp.int32

EPS = 1e-6
GATE_SOFTCAP = 15.0
MOE_TOPK = 2

LANES = 128
S5_CHUNK = 16
S5_PAIR = 2
ML_CHUNK = 256
MOE_ROWS_LOG2 = 7
MOE_ROWS = 1 << MOE_ROWS_LOG2
VMEM_BYTES = 64 * 1024 * 1024
HIGHEST = lax.Precision.HIGHEST


def _params(sem, vmem_mib):
    assert vmem_mib * 1024 * 1024 < VMEM_BYTES
    return pltpu.CompilerParams(dimension_semantics=sem, vmem_limit_bytes=vmem_mib * 1024 * 1024)


def _rms(x32, w):
    ms = jnp.mean(x32 * x32, axis=-1, keepdims=True)
    return x32 * lax.rsqrt(ms + EPS) * w


def _sigmoid(x):
    return 1.0 / (1.0 + jnp.exp(-x))


def _gelu_tanh(x):
    c = math.sqrt(2.0 / math.pi)
    return 0.5 * x * (1.0 + jnp.tanh(c * (x + 0.044715 * (x * x * x))))


def _rms_cast_kernel(x_ref, w_ref, o_ref):
    o_ref[...] = _rms(x_ref[...], w_ref[...]).astype(o_ref.dtype)


def _rms_cast(x2, w, tm=512):
    t, d = x2.shape
    return pl.pallas_call(
        _rms_cast_kernel,
        grid=(t // tm,),
        in_specs=[pl.BlockSpec((tm, d), lambda i: (i, 0)), pl.BlockSpec((1, d), lambda i: (0, 0))],
        out_specs=pl.BlockSpec((tm, d), lambda i: (i, 0)),
        out_shape=jax.ShapeDtypeStruct((t, d), BF16),
        compiler_params=_params(("parallel",), 32),
        name="rms_cast",
    )(x2, w.reshape(1, d))


def _s5_kernel(u_ref, win_ref, toep_ref, wout_ref, lpow_ref, y_ref, *, n_chunks):
    u = u_ref[0]
    v = jnp.dot(u, win_ref[0], preferred_element_type=F32)
    rows, width = v.shape
    half = width // 2
    assert n_chunks & (n_chunks - 1) == 0
    cidx = lax.broadcasted_iota(I32, (rows, 1), 0) & (n_chunks - 1)

    def shifted(a, d):
        return jnp.where(cidx >= d, pltpu.roll(a, d, 0), 0.0)

    xr = shifted(v[:, :half], 1)
    xi = shifted(v[:, half:], 1)
    d, k = 1, 0
    while d < n_chunks:
        ar = lpow_ref[0, k:k + 1, :half]
        ai = lpow_ref[0, k:k + 1, half:]
        sr = shifted(xr, d)
        si = shifted(xi, d)
        xr, xi = xr + (ar * sr - ai * si), xi + (ar * si + ai * sr)
        d, k = 2 * d, k + 1
    x = jnp.concatenate([xr, xi], axis=1).astype(BF16)
    y = (jnp.dot(u, toep_ref[0], preferred_element_type=F32)
         + jnp.dot(x, wout_ref[0], preferred_element_type=F32))
    y_ref[0] = y.astype(y_ref.dtype)


def _s5_operators(lam_re, lam_im, log_dt, b_re, b_im, c_re, c_im, n_chunks):
    g, p, cg = b_re.shape
    L = S5_CHUNK
    lam = lax.complex(lam_re.astype(F32), lam_im.astype(F32))
    dt = jnp.exp(log_dt.astype(F32))[:, None]
    z = lam * dt
    lam_bar = jnp.exp(z)
    b_bar = ((lam_bar - 1.0) / lam)[..., None] * lax.complex(b_re.astype(F32), b_im.astype(F32))
    cc = lax.complex(c_re.astype(F32), c_im.astype(F32))
    steps = jnp.arange(L + 1, dtype=F32)
    pw = jnp.exp(z[:, None, :] * steps[None, :, None])

    win = pw[:, L - 1 - jnp.arange(L), :][:, :, None, :] * jnp.transpose(b_bar, (0, 2, 1))[:, None, :, :]
    win = win.reshape(g, L * cg, p)
    wout = (jnp.transpose(pw[:, 1:L + 1, :], (0, 2, 1))[:, :, :, None]
            * jnp.transpose(cc, (0, 2, 1))[:, :, None, :]).reshape(g, p, L * cg)
    m1 = cc[:, None, :, :] * pw[:, :L, None, :]
    kk = (jnp.einsum('gkop,gpi->gkoi', jnp.real(m1), jnp.real(b_bar), precision=HIGHEST)
          - jnp.einsum('gkop,gpi->gkoi', jnp.imag(m1), jnp.imag(b_bar), precision=HIGHEST))
    s_idx = jnp.arange(L)[:, None]
    t_idx = jnp.arange(L)[None, :]
    lag = jnp.clip(t_idx - s_idx, 0, L - 1)
    toep = kk[:, lag]
    toep = jnp.where((t_idx >= s_idx)[None, :, :, None, None], toep, 0.0)
    toep = jnp.transpose(toep, (0, 1, 4, 2, 3)).reshape(g, L * cg, L * cg)

    n_steps = max(1, (n_chunks - 1).bit_length())
    hops = (2.0 ** jnp.arange(n_steps, dtype=F32)) * L
    lp = jnp.exp(z[:, None, :] * hops[None, :, None])

    gp = g // S5_PAIR
    k_in, k_st = L * cg, p

    def pair(a):
        return a.reshape((gp, S5_PAIR) + a.shape[1:])

    win_re, win_im = pair(jnp.real(win)), pair(jnp.imag(win))
    win2 = jnp.zeros((gp, S5_PAIR, k_in, 2, S5_PAIR, k_st), F32)
    wout_re, wout_im = pair(jnp.real(wout)), pair(-jnp.imag(wout))
    wout2 = jnp.zeros((gp, 2, S5_PAIR, k_st, S5_PAIR, k_in), F32)
    toep_p = pair(toep)
    toep2 = jnp.zeros((gp, S5_PAIR, k_in, S5_PAIR, k_in), F32)
    for j in range(S5_PAIR):
        win2 = win2.at[:, j, :, 0, j, :].set(win_re[:, j]).at[:, j, :, 1, j, :].set(win_im[:, j])
        wout2 = wout2.at[:, 0, j, :, j, :].set(wout_re[:, j]).at[:, 1, j, :, j, :].set(wout_im[:, j])
        toep2 = toep2.at[:, j, :, j, :].set(toep_p[:, j])
    win2 = win2.reshape(gp, S5_PAIR * k_in, 2 * S5_PAIR * k_st).astype(BF16)
    wout2 = wout2.reshape(gp, 2 * S5_PAIR * k_st, S5_PAIR * k_in).astype(BF16)
    toep2 = toep2.reshape(gp, S5_PAIR * k_in, S5_PAIR * k_in).astype(BF16)
    lp_p = pair(lp)
    lpow = jnp.concatenate([jnp.transpose(jnp.real(lp_p), (0, 2, 1, 3)).reshape(gp, n_steps, S5_PAIR * k_st),
                            jnp.transpose(jnp.imag(lp_p), (0, 2, 1, 3)).reshape(gp, n_steps, S5_PAIR * k_st)],
                           axis=-1)
    return win2, toep2, wout2, lpow


def _s5_mix(h_bf, bsz, seq, ops):
    win2, toep2, wout2, lpow = ops
    gp, k2, st2 = win2.shape
    d = h_bf.shape[-1]
    L = S5_CHUNK
    cg = k2 // (S5_PAIR * L)
    n_chunks = seq // L
    rows = bsz * n_chunks
    u = h_bf.reshape(bsz, n_chunks, L, gp, S5_PAIR, cg)
    u = jnp.transpose(u, (3, 0, 1, 4, 2, 5)).reshape(gp, rows, k2)
    y = pl.pallas_call(
        functools.partial(_s5_kernel, n_chunks=n_chunks),
        grid=(gp,),
        in_specs=[pl.BlockSpec((1, rows, k2), lambda i: (i, 0, 0)),
                  pl.BlockSpec((1, k2, st2), lambda i: (i, 0, 0)),
                  pl.BlockSpec((1, k2, k2), lambda i: (i, 0, 0)),
                  pl.BlockSpec((1, st2, k2), lambda i: (i, 0, 0)),
                  pl.BlockSpec((1,) + lpow.shape[1:], lambda i: (i, 0, 0))],
        out_specs=pl.BlockSpec((1, rows, k2), lambda i: (i, 0, 0)),
        out_shape=jax.ShapeDtypeStruct((gp, rows, k2), BF16),
        compiler_params=_params(("parallel",), 40),
        name="s5_scan",
    )(u, win2, toep2, wout2, lpow)
    y = y.reshape(gp, bsz, n_chunks, S5_PAIR, L, cg)
    return jnp.transpose(y, (1, 2, 4, 0, 3, 5)).reshape(bsz * seq, d)


def _glu_kernel(x_ref, y_ref, nw_ref, dsk_ref, wv_ref, wg_ref, xres_ref, o_ref, g_scr):
    @pl.when(pl.program_id(1) == 0)
    def _():
        h = _rms(x_ref[...], nw_ref[...])
        g_scr[...] = _gelu_tanh(y_ref[...].astype(F32) + dsk_ref[...] * h).astype(BF16)

    g = g_scr[...]
    val = jnp.dot(g, wv_ref[...], preferred_element_type=F32)
    gate = jnp.dot(g, wg_ref[...], preferred_element_type=F32)
    o_ref[...] = xres_ref[...] + val * _sigmoid(gate)


def _glu(x2, y_bf, nw, d_skip, w_glu_bf, tm=512, tn=512):
    t, d = x2.shape
    nj = d // tn
    return pl.pallas_call(
        _glu_kernel,
        grid=(t // tm, nj),
        in_specs=[pl.BlockSpec((tm, d), lambda i, j: (i, 0)),
                  pl.BlockSpec((tm, d), lambda i, j: (i, 0)),
                  pl.BlockSpec((1, d), lambda i, j: (0, 0)),
                  pl.BlockSpec((1, d), lambda i, j: (0, 0)),
                  pl.BlockSpec((d, tn), lambda i, j: (0, j)),
                  pl.BlockSpec((d, tn), lambda i, j: (0, j + nj)),
                  pl.BlockSpec((tm, tn), lambda i, j: (i, j))],
        out_specs=pl.BlockSpec((tm, tn), lambda i, j: (i, j)),
        out_shape=jax.ShapeDtypeStruct((t, d), F32),
        scratch_shapes=[pltpu.VMEM((tm, d), BF16)],
        compiler_params=_params(("parallel", "arbitrary"), 48),
        name="s5_glu",
    )(x2, y_bf, nw.reshape(1, d), d_skip.reshape(1, d), w_glu_bf, w_glu_bf, x2)


def _inproj_kernel(x_ref, nw_ref, w_ref, z_ref, h_scr):
    @pl.when(pl.program_id(1) == 0)
    def _():
        h_scr[...] = _rms(x_ref[...], nw_ref[...]).astype(BF16)

    z_ref[...] = jnp.dot(h_scr[...], w_ref[...], preferred_element_type=F32).astype(z_ref.dtype)


def _inproj(x2, nw, w_bf, tm=1024, tn=512):
    t, d = x2.shape
    n = w_bf.shape[1]
    return pl.pallas_call(
        _inproj_kernel,
        grid=(t // tm, n // tn),
        in_specs=[pl.BlockSpec((tm, d), lambda i, j: (i, 0)),
                  pl.BlockSpec((1, d), lambda i, j: (0, 0)),
                  pl.BlockSpec((d, tn), lambda i, j: (0, j))],
        out_specs=pl.BlockSpec((tm, tn), lambda i, j: (i, j)),
        out_shape=jax.ShapeDtypeStruct((t, n), BF16),
        scratch_shapes=[pltpu.VMEM((tm, d), BF16)],
        compiler_params=_params(("parallel", "arbitrary"), 48),
        name="ml_inproj",
    )(x2, nw.reshape(1, d), w_bf)


def _softcap_gates(pre, is_forget):
    t = GATE_SOFTCAP * jnp.tanh(pre / GATE_SOFTCAP)
    log_sig = jnp.minimum(t, 0.0) - jnp.log(1.0 + jnp.exp(-jnp.abs(t)))
    return jnp.where(is_forget, log_sig, t)


def _gates_kernel(x_ref, nw_ref, wc_ref, wr_ref, bc_ref, br_ref, gc_ref, gr_ref, *, n_heads):
    h = _rms(x_ref[...], nw_ref[...])
    pre_c = jnp.dot(h, wc_ref[...], preferred_element_type=F32, precision=HIGHEST) + bc_ref[...]
    pre_r = lax.dot_general(wr_ref[...], h, (((1,), (1,)), ((), ())),
                            preferred_element_type=F32, precision=HIGHEST) + br_ref[...]
    gc_ref[...] = _softcap_gates(pre_c, lax.broadcasted_iota(I32, pre_c.shape, 1) >= n_heads)
    gr_ref[...] = _softcap_gates(pre_r, lax.broadcasted_iota(I32, pre_r.shape, 0) >= n_heads)


def _gates(x2, nw, w_gate, b_gate, tm=512):
    t, d = x2.shape
    ng = w_gate.shape[1]
    return pl.pallas_call(
        functools.partial(_gates_kernel, n_heads=ng // 2),
        grid=(t // tm,),
        in_specs=[pl.BlockSpec((tm, d), lambda i: (i, 0)),
                  pl.BlockSpec((1, d), lambda i: (0, 0)),
                  pl.BlockSpec((d, ng), lambda i: (0, 0)),
                  pl.BlockSpec((ng, d), lambda i: (0, 0)),
                  pl.BlockSpec((1, ng), lambda i: (0, 0)),
                  pl.BlockSpec((ng, 1), lambda i: (0, 0))],
        out_specs=[pl.BlockSpec((tm, ng), lambda i: (i, 0)), pl.BlockSpec((ng, tm), lambda i: (0, i))],
        out_shape=[jax.ShapeDtypeStruct((t, ng), F32), jax.ShapeDtypeStruct((ng, t), F32)],
        compiler_params=_params(("parallel",), 32),
        name="ml_gates",
    )(x2, nw.reshape(1, d), w_gate, w_gate.T, b_gate.reshape(1, ng), b_gate.reshape(ng, 1))


def _mlstm_kernel(q_ref, k_ref, v_ref, o_ref, gc_ref, gr_ref, gh_ref, out_ref, c_scr, n_scr, m_scr,
                  *, n_heads, scale):
    head = pl.program_id(1)

    @pl.when(pl.program_id(2) == 0)
    def _():
        c_scr[...] = jnp.zeros_like(c_scr)
        n_scr[...] = jnp.zeros_like(n_scr)
        m_scr[...] = jnp.zeros_like(m_scr)

    gc = gc_ref[...]
    gr = gr_ref[...]
    lane_h = lax.broadcasted_iota(I32, gc.shape, 1)
    sub_h = lax.broadcasted_iota(I32, gr.shape, 0)
    i_col = jnp.sum(jnp.where(lane_h == head, gc, 0.0), axis=1, keepdims=True)
    f_col = jnp.sum(jnp.where(lane_h == head + n_heads, gc, 0.0), axis=1, keepdims=True)
    i_row = jnp.sum(jnp.where(sub_h == head, gr, 0.0), axis=0, keepdims=True)
    f_row = jnp.sum(jnp.where(sub_h == head + n_heads, gr, 0.0), axis=0, keepdims=True)

    L = gc.shape[0]
    t_idx = lax.broadcasted_iota(I32, (L, L), 0)
    s_idx = lax.broadcasted_iota(I32, (L, L), 1)
    causal = s_idx <= t_idx
    bcum_col = jnp.sum(jnp.where(causal, f_row, 0.0), axis=1, keepdims=True)
    bcum_row = jnp.sum(jnp.where(t_idx <= s_idx, f_col, 0.0), axis=0, keepdims=True)

    m_prev = m_scr[0:1, 0:1]
    a_log = bcum_col + m_prev
    d_log = jnp.where(causal, bcum_col - bcum_row + i_row, -jnp.inf)
    m_t = jnp.maximum(a_log, jnp.max(d_log, axis=1, keepdims=True))
    w_intra = jnp.exp(d_log - m_t)
    w_inter = jnp.exp(a_log - m_t)

    q = q_ref[...]
    k = k_ref[...]
    v = v_ref[...]
    q32 = q.astype(F32)
    k32 = k.astype(F32)
    c_mat = c_scr[...]
    n_row = n_scr[...]
    sc = lax.dot_general(q, k, (((1,), (1,)), ((), ())), preferred_element_type=F32) * (w_intra * scale)
    num = (w_inter * jnp.dot(q, c_mat.astype(BF16), preferred_element_type=F32)
           + jnp.dot(sc.astype(BF16), v, preferred_element_type=F32))
    den = (w_inter * jnp.sum(q32 * n_row, axis=1, keepdims=True) + jnp.sum(sc, axis=1, keepdims=True))
    h_out = num / jnp.maximum(jnp.abs(den), jnp.exp(-m_t))

    b_last = bcum_col[L - 1:L, :]
    g_log = b_last - bcum_col + i_col
    m_new = jnp.maximum(b_last + m_prev, jnp.max(g_log, axis=0, keepdims=True))
    decay = jnp.exp(b_last + m_prev - m_new)
    wk = jnp.exp(g_log - m_new) * scale
    wv = (wk * v.astype(F32)).astype(BF16)
    kv = lax.dot_general(k, wv, (((0,), (0,)), ((), ())), preferred_element_type=F32)
    c_scr[...] = decay * c_mat + kv
    n_scr[...] = decay * n_row + jnp.sum(wk * k32, axis=0, keepdims=True)
    m_scr[...] = jnp.broadcast_to(m_new, m_scr.shape)

    hn = h_out * lax.rsqrt(jnp.mean(h_out * h_out, axis=-1, keepdims=True) + EPS) * gh_ref[...]
    out_ref[...] = (_sigmoid(o_ref[...].astype(F32)) * hn).astype(out_ref.dtype)


def _mlstm(z, gcol, grow, g_head, bsz, seq, n_heads, dk, dv):
    t = bsz * seq
    L = ML_CHUNK
    nc = seq // L
    qk_w = n_heads * dk
    ng = 2 * n_heads
    k_off = qk_w // dk
    v_off = (2 * qk_w) // dv
    o_off = (2 * qk_w + n_heads * dv) // dv
    return pl.pallas_call(
        functools.partial(_mlstm_kernel, n_heads=n_heads, scale=dk ** -0.5),
        grid=(bsz, n_heads, nc),
        in_specs=[pl.BlockSpec((L, dk), lambda b, h, c: (b * nc + c, h)),
                  pl.BlockSpec((L, dk), lambda b, h, c: (b * nc + c, k_off + h)),
                  pl.BlockSpec((L, dv), lambda b, h, c: (b * nc + c, v_off + h)),
                  pl.BlockSpec((L, dv), lambda b, h, c: (b * nc + c, o_off + h)),
                  pl.BlockSpec((L, ng), lambda b, h, c: (b * nc + c, 0)),
                  pl.BlockSpec((ng, L), lambda b, h, c: (0, b * nc + c)),
                  pl.BlockSpec((1, dv), lambda b, h, c: (0, h))],
        out_specs=pl.BlockSpec((L, dv), lambda b, h, c: (b * nc + c, h)),
        out_shape=jax.ShapeDtypeStruct((t, n_heads * dv), BF16),
        scratch_shapes=[pltpu.VMEM((dk, dv), F32), pltpu.VMEM((1, dk), F32), pltpu.VMEM((1, LANES), F32)],
        compiler_params=_params(("parallel", "parallel", "arbitrary"), 32),
        name="ml_chunk",
    )(z, z, z, z, gcol, grow, g_head.reshape(1, n_heads * dv))


def _outproj_kernel(a_ref, w_ref, xres_ref, o_ref):
    o_ref[...] = xres_ref[...] + jnp.dot(a_ref[...], w_ref[...], preferred_element_type=F32)


def _outproj(a_bf, w_bf, x2, tm=1024, tn=512):
    t, kdim = a_bf.shape
    d = w_bf.shape[1]
    return pl.pallas_call(
        _outproj_kernel,
        grid=(t // tm, d // tn),
        in_specs=[pl.BlockSpec((tm, kdim), lambda i, j: (i, 0)),
                  pl.BlockSpec((kdim, tn), lambda i, j: (0, j)),
                  pl.BlockSpec((tm, tn), lambda i, j: (i, j))],
        out_specs=pl.BlockSpec((tm, tn), lambda i, j: (i, j)),
        out_shape=jax.ShapeDtypeStruct((t, d), F32),
        compiler_params=_params(("parallel", "arbitrary"), 40),
        name="ml_outproj",
    )(a_bf, w_bf, x2)


def _route_kernel(x_ref, nw_ref, wt_ref, b_ref, h_ref, eidx_ref, gate_ref, *, n_groups, per_group, seg):
    h = _rms(x_ref[...], nw_ref[...])
    tm = h.shape[0]
    for s in range(seg):
        h_ref[pl.ds(s, tm, stride=seg), :] = h[:, s * LANES:(s + 1) * LANES]

    logits = lax.dot_general(wt_ref[...], h, (((1,), (1,)), ((), ())),
                             preferred_element_type=F32, precision=HIGHEST) + b_ref[...]
    n_exp = n_groups * per_group
    lg = logits[0:n_groups, :]
    le = logits[n_groups:n_groups + n_exp, :]
    g_iota = lax.broadcasted_iota(I32, lg.shape, 0)
    g_max = jnp.max(lg, axis=0, keepdims=True)
    g_sel = jnp.min(jnp.where(lg == g_max, g_iota, n_groups), axis=0, keepdims=True)
    p_sel = 1.0 / jnp.sum(jnp.exp(lg - g_max), axis=0, keepdims=True)

    e_iota = lax.broadcasted_iota(I32, le.shape, 0)
    lo = g_sel * per_group
    cand = jnp.where((e_iota >= lo) & (e_iota < lo + per_group), le, -jnp.inf)
    v1 = jnp.max(cand, axis=0, keepdims=True)
    i1 = jnp.min(jnp.where(cand == v1, e_iota, n_exp), axis=0, keepdims=True)
    cand2 = jnp.where(e_iota == i1, -jnp.inf, cand)
    v2 = jnp.max(cand2, axis=0, keepdims=True)
    i2 = jnp.min(jnp.where(cand2 == v2, e_iota, n_exp), axis=0, keepdims=True)
    e2 = jnp.exp(v2 - v1)
    inv = p_sel / (1.0 + e2)
    eidx_ref[...] = jnp.concatenate([i1, i2], axis=0)
    gate_ref[...] = jnp.concatenate([inv, e2 * inv], axis=0)


def _route(x2, nw, w_group, b_group, w_expert, b_expert, tm=512):
    t, d = x2.shape
    seg = d // LANES
    n_groups = w_group.shape[1]
    n_exp = w_expert.shape[1]
    n_rows = -(-(n_groups + n_exp) // LANES) * LANES
    wt = jnp.zeros((n_rows, d), F32).at[:n_groups].set(w_group.T.astype(F32))
    wt = wt.at[n_groups:n_groups + n_exp].set(w_expert.T.astype(F32))
    bias = jnp.zeros((n_rows, 1), F32).at[:n_groups, 0].set(b_group.astype(F32))
    bias = bias.at[n_groups:n_groups + n_exp, 0].set(b_expert.astype(F32))
    return pl.pallas_call(
        functools.partial(_route_kernel, n_groups=n_groups, per_group=n_exp // n_groups, seg=seg),
        grid=(t // tm,),
        in_specs=[pl.BlockSpec((tm, d), lambda i: (i, 0)),
                  pl.BlockSpec((1, d), lambda i: (0, 0)),
                  pl.BlockSpec((n_rows, d), lambda i: (0, 0)),
                  pl.BlockSpec((n_rows, 1), lambda i: (0, 0))],
        out_specs=[pl.BlockSpec((tm * seg, LANES), lambda i: (i, 0)),
                   pl.BlockSpec((MOE_TOPK, tm), lambda i: (0, i)),
                   pl.BlockSpec((MOE_TOPK, tm), lambda i: (0, i))],
        out_shape=[jax.ShapeDtypeStruct((t * seg, LANES), F32),
                   jax.ShapeDtypeStruct((MOE_TOPK, t), I32),
                   jax.ShapeDtypeStruct((MOE_TOPK, t), F32)],
        compiler_params=_params(("parallel",), 40),
        name="moe_route",
    )(x2, nw.reshape(1, d), wt, bias)


def _meta_kernel(eidx_ref, dest_ref, bexp_ref, nused_ref, *, n_exp, n_tiles):
    tl = eidx_ref.shape[1]
    e_iota = lax.broadcasted_iota(I32, (n_exp, tl), 0)
    upper = (lax.broadcasted_iota(I32, (tl, tl), 0) < lax.broadcasted_iota(I32, (tl, tl), 1)).astype(BF16)

    def hits(i):
        return [e_iota == eidx_ref[pl.ds(k * n_tiles + i, 1), :] for k in range(MOE_TOPK)]

    def rank_tile(i, carry):
        hit = hits(i)
        member = sum(h.astype(F32) for h in hit)
        before = jnp.dot(member.astype(BF16), upper, preferred_element_type=F32) + carry
        for k in range(MOE_TOPK):
            rank = jnp.sum(jnp.where(hit[k], before, 0.0), axis=0, keepdims=True)
            dest_ref[pl.ds(k * n_tiles + i, 1), :] = rank.astype(I32)
        return carry + jnp.sum(member, axis=1, keepdims=True)

    counts = lax.fori_loop(0, n_tiles, rank_tile, jnp.zeros((n_exp, 1), F32)).astype(I32)
    padded = ((counts + (MOE_ROWS - 1)) >> MOE_ROWS_LOG2) << MOE_ROWS_LOG2
    lower = (lax.broadcasted_iota(I32, (n_exp, n_exp), 1) < lax.broadcasted_iota(I32, (n_exp, n_exp), 0))
    pad_start = jnp.dot(lower.astype(F32), jnp.broadcast_to(padded.astype(F32), (n_exp, LANES)),
                        preferred_element_type=F32, precision=HIGHEST)[:, 0:1].astype(I32)
    pad_end = pad_start + padded

    def dest_tile(i, carry):
        hit = hits(i)
        for k in range(MOE_TOPK):
            base = jnp.sum(jnp.where(hit[k], pad_start, 0), axis=0, keepdims=True)
            row = pl.ds(k * n_tiles + i, 1)
            dest_ref[row, :] = dest_ref[row, :] + base
        return carry

    lax.fori_loop(0, n_tiles, dest_tile, 0)
    blk_start = lax.broadcasted_iota(I32, (n_exp, bexp_ref.shape[1]), 1) * MOE_ROWS
    bexp = jnp.sum((pad_end <= blk_start).astype(I32), axis=0, keepdims=True)
    bexp_ref[...] = jnp.minimum(bexp, n_exp - 1)
    total = jnp.sum(padded, axis=0, keepdims=True)
    nused_ref[...] = jnp.broadcast_to(total >> MOE_ROWS_LOG2, nused_ref.shape)


def _meta(eidx, n_exp, n_blocks, tl=512):
    n_tok = eidx.shape[1]
    n_tiles = n_tok // tl
    nb_pad = -(-n_blocks // LANES) * LANES
    dest, bexp, nused = pl.pallas_call(
        functools.partial(_meta_kernel, n_exp=n_exp, n_tiles=n_tiles),
        out_shape=[jax.ShapeDtypeStruct((MOE_TOPK * n_tiles, tl), I32),
                   jax.ShapeDtypeStruct((1, nb_pad), I32),
                   jax.ShapeDtypeStruct((1, LANES), I32)],
        compiler_params=pltpu.CompilerParams(vmem_limit_bytes=32 * 1024 * 1024),
        name="moe_meta",
    )(eidx.reshape(MOE_TOPK * n_tiles, tl))
    return dest.reshape(MOE_TOPK, n_tok), bexp, nused


def _row_copy(src_ref, src_row, dst_ref, dst_row, sem, seg):
    return pltpu.make_async_copy(src_ref.at[pl.ds(pl.multiple_of(src_row * seg, seg), seg), :],
                                 dst_ref.at[pl.ds(pl.multiple_of(dst_row * seg, seg), seg), :], sem)


def _dispatch_kernel(d0_ref, d1_ref, h_ref, xs_in_ref, xs_ref, sem, *, tt, seg):
    del xs_in_ref
    base = pl.program_id(0) * tt
    dests = (d0_ref, d1_ref)

    def start(t, c):
        for dref in dests:
            _row_copy(h_ref, t, xs_ref, dref[base + t], sem, seg).start()
        return c

    def wait(t, c):
        for dref in dests:
            _row_copy(h_ref, t, xs_ref, dref[base + t], sem, seg).wait()
        return c

    lax.fori_loop(0, tt, start, 0)
    lax.fori_loop(0, tt, wait, 0)


def _dispatch(h_rows, dest, n_slots, tt=1024):
    seg = h_rows.shape[0] // dest.shape[1]
    n_tok = dest.shape[1]
    xs0 = jnp.zeros((n_slots * seg, LANES), h_rows.dtype)
    return pl.pallas_call(
        functools.partial(_dispatch_kernel, tt=tt, seg=seg),
        grid_spec=pltpu.PrefetchScalarGridSpec(
            num_scalar_prefetch=2,
            grid=(n_tok // tt,),
            in_specs=[pl.BlockSpec((tt * seg, LANES), lambda i, d0, d1: (i, 0)),
                      pl.BlockSpec(memory_space=pl.ANY)],
            out_specs=pl.BlockSpec(memory_space=pl.ANY),
            scratch_shapes=[pltpu.SemaphoreType.DMA(())]),
        out_shape=jax.ShapeDtypeStruct(xs0.shape, xs0.dtype),
        input_output_aliases={3: 0},
        compiler_params=_params(("arbitrary",), 40),
        name="moe_dispatch",
    )(dest[0], dest[1], h_rows, xs0)


def _expert_kernel(bexp_ref, nused_ref, x_ref, wgu_ref, wdn_ref, y_ref, x_scr, wgu_scr, wdn_scr,
                   *, seg, hidden):
    r = pl.program_id(0)
    rows = x_scr.shape[0]
    prev = bexp_ref[jnp.maximum(r - 1, 0)]

    @pl.when((r == 0) | (bexp_ref[r] != prev))
    def _():
        wgu_scr[...] = wgu_ref[0, 0].astype(BF16)
        wdn_scr[...] = wdn_ref[0, 0].astype(BF16)

    @pl.when(r < nused_ref[0])
    def _():
        for s in range(seg):
            x_scr[:, s * LANES:(s + 1) * LANES] = x_ref[pl.ds(s, rows, stride=seg), :].astype(BF16)
        gu = jnp.dot(x_scr[...], wgu_scr[...], preferred_element_type=F32)
        g = gu[:, :hidden]
        act = (g * _sigmoid(g)) * gu[:, hidden:]
        y = jnp.dot(act.astype(BF16), wdn_scr[...], preferred_element_type=F32)
        for s in range(seg):
            y_ref[pl.ds(s, rows, stride=seg), :] = y[:, s * LANES:(s + 1) * LANES]

    @pl.when(r >= nused_ref[0])
    def _():
        y_ref[...] = jnp.zeros_like(y_ref)


def _experts(xs, bexp, nused, layer, w_gu, w_down, n_blocks):
    d, two_f = w_gu.shape[2], w_gu.shape[3]
    hidden = two_f // 2
    seg = d // LANES
    rows = MOE_ROWS
    return pl.pallas_call(
        functools.partial(_expert_kernel, seg=seg, hidden=hidden),
        grid_spec=pltpu.PrefetchScalarGridSpec(
            num_scalar_prefetch=2,
            grid=(n_blocks,),
            in_specs=[pl.BlockSpec((rows * seg, LANES), lambda r, be, nu: (r, 0)),
                      pl.BlockSpec((1, 1, d, two_f), lambda r, be, nu: (layer, be[r], 0, 0)),
                      pl.BlockSpec((1, 1, hidden, d), lambda r, be, nu: (layer, be[r], 0, 0))],
            out_specs=pl.BlockSpec((rows * seg, LANES), lambda r, be, nu: (r, 0)),
            scratch_shapes=[pltpu.VMEM((rows, d), BF16), pltpu.VMEM((d, two_f), BF16),
                            pltpu.VMEM((hidden, d), BF16)]),
        out_shape=jax.ShapeDtypeStruct(xs.shape, F32),
        compiler_params=_params(("arbitrary",), 58),
        name="moe_experts",
    )(bexp, nused, xs, w_gu, w_down)


def _combine_kernel(d0_ref, d1_ref, x_ref, g_ref, nw_ref, ys_ref, o_ref, buf, sem, *, tt, seg, final_norm):
    base = pl.program_id(0) * tt
    dests = (d0_ref, d1_ref)

    def start(t, c):
        for k, dref in enumerate(dests):
            _row_copy(ys_ref, dref[base + t], buf.at[k], t, sem, seg).start()
        return c

    def wait(t, c):
        for k, dref in enumerate(dests):
            _row_copy(ys_ref, dref[base + t], buf.at[k], t, sem, seg).wait()
        return c

    lax.fori_loop(0, tt, start, 0)
    lax.fori_loop(0, tt, wait, 0)
    g = g_ref[...]
    for s in range(seg):
        cols = slice(s * LANES, (s + 1) * LANES)
        mix = sum(g[:, k:k + 1] * buf[k, pl.ds(s, tt, stride=seg), :] for k in range(MOE_TOPK))
        o_ref[:, cols] = x_ref[:, cols] + mix
    if final_norm:
        o_ref[...] = _rms(o_ref[...], nw_ref[...])


def _combine(x2, ys, dest, gates_t, final_w, tt=256):
    t, d = x2.shape
    seg = d // LANES
    nw = jnp.ones((1, d), F32) if final_w is None else final_w.reshape(1, d).astype(F32)
    return pl.pallas_call(
        functools.partial(_combine_kernel, tt=tt, seg=seg, final_norm=final_w is not None),
        grid_spec=pltpu.PrefetchScalarGridSpec(
            num_scalar_prefetch=2,
            grid=(t // tt,),
            in_specs=[pl.BlockSpec((tt, d), lambda i, d0, d1: (i, 0)),
                      pl.BlockSpec((tt, MOE_TOPK), lambda i, d0, d1: (i, 0)),
                      pl.BlockSpec((1, d), lambda i, d0, d1: (0, 0)),
                      pl.BlockSpec(memory_space=pl.ANY)],
            out_specs=pl.BlockSpec((tt, d), lambda i, d0, d1: (i, 0)),
            scratch_shapes=[pltpu.VMEM((MOE_TOPK, tt * seg, LANES), F32), pltpu.SemaphoreType.DMA(())]),
        out_shape=jax.ShapeDtypeStruct((t, d), F32),
        compiler_params=_params(("arbitrary",), 40),
        name="moe_combine",
    )(dest[0], dest[1], x2, gates_t, nw, ys)


def _hier_moe(x2, layer, nw, w_group, b_group, w_expert, b_expert, w_gu, w_down, final_w):
    t = x2.shape[0]
    n_exp = w_expert.shape[1]
    n_blocks = (t * MOE_TOPK) // MOE_ROWS + n_exp
    h_rows, eidx, gates = _route(x2, nw, w_group, b_group, w_expert, b_expert)
    dest, bexp, nused = _meta(eidx, n_exp, n_blocks)
    xs = _dispatch(h_rows, dest, n_blocks * MOE_ROWS)
    ys = _experts(xs, bexp[0], nused[0, :1], layer, w_gu, w_down, n_blocks)
    return _combine(x2, ys, dest, gates.T, final_w)


def kernel(x, s5_lam_re, s5_lam_im, s5_log_dt, s5_b_re, s5_b_im, s5_c_re, s5_c_im, s5_d, s5_w_glu, ml_w_in, ml_b_gate, ml_g_head, ml_w_out, norm_mix, norm_ffn, moe_w_group, moe_b_group, moe_w_expert, moe_b_expert, moe_w_gu, moe_w_down, norm_final):
    bsz, seq, d = x.shape
    assert norm_mix.shape[0] == 2 and s5_w_glu.shape[0] == 1 and ml_w_in.shape[0] == 1
    x2 = x.reshape(bsz * seq, d).astype(F32)

    ops = _s5_operators(s5_lam_re[0], s5_lam_im[0], s5_log_dt[0], s5_b_re[0], s5_b_im[0],
                        s5_c_re[0], s5_c_im[0], seq // S5_CHUNK)
    h_bf = _rms_cast(x2, norm_mix[0])
    y_bf = _s5_mix(h_bf, bsz, seq, ops)
    x2 = _glu(x2, y_bf, norm_mix[0], s5_d[0], s5_w_glu[0].astype(BF16))
    x2 = _hier_moe(x2, 0, norm_ffn[0], moe_w_group[0], moe_b_group[0], moe_w_expert[0], moe_b_expert[0],
                   moe_w_gu, moe_w_down, None)

    n_heads = ml_b_gate.shape[1] // 2
    dv = ml_w_out.shape[1] // n_heads
    dk = (ml_w_in.shape[2] - 2 * n_heads * dv - 2 * n_heads) // (2 * n_heads)
    n_main = 2 * n_heads * (dk + dv)
    z = _inproj(x2, norm_mix[1], ml_w_in[0, :, :n_main].astype(BF16))
    gcol, grow = _gates(x2, norm_mix[1], ml_w_in[0, :, n_main:].astype(F32), ml_b_gate[0].astype(F32))
    a_bf = _mlstm(z, gcol, grow, ml_g_head[0].astype(F32), bsz, seq, n_heads, dk, dv)
    x2 = _outproj(a_bf, ml_w_out[0].astype(BF16), x2)
    x2 = _hier_moe(x2, 1, norm_ffn[1], moe_w_group[1], moe_b_group[1], moe_w_expert[1], moe_b_expert[1],
                   moe_w_gu, moe_w_down, norm_final)
    return x2.reshape(bsz, seq, d).astype(x.dtype)
```

```python
import functools
import math

import jax
import jax.numpy as jnp
from jax import lax
from jax.experimental import pallas as pl
from jax.experimental.pallas import tpu as pltpu

F32 = jnp.float32
BF16 = jnp.bfloat16
I32 = jnp.int32

EPS = 1e-6
GATE_SOFTCAP = 15.0
MOE_TOPK = 2

LANES = 128
S5_CHUNK = 16
ML_CHUNK = 256
MOE_ROWS_LOG2 = 8
MOE_ROWS = 1 << MOE_ROWS_LOG2
VMEM_BYTES = 64 * 1024 * 1024
DMA_UNROLL = 8
HIGHEST = lax.Precision.HIGHEST


def _params(sem, vmem_mib):
    assert vmem_mib * 1024 * 1024 < VMEM_BYTES
    return pltpu.CompilerParams(dimension_semantics=sem, vmem_limit_bytes=vmem_mib * 1024 * 1024)


def _rms(x32, w):
    ms = jnp.mean(x32 * x32, axis=-1, keepdims=True)
    return x32 * lax.rsqrt(ms + EPS) * w


def _sigmoid(x):
    return 1.0 / (1.0 + jnp.exp(-x))


def _gelu_tanh(x):
    c = math.sqrt(2.0 / math.pi)
    return 0.5 * x * (1.0 + jnp.tanh(c * (x + 0.044715 * (x * x * x))))


def _row_scale_kernel(x_ref, r_ref):
    x = x_ref[...]
    r_ref[...] = lax.rsqrt(jnp.mean(x * x, axis=-1, keepdims=True) + EPS)


def _row_scale(x2, tm=512):
    t, d = x2.shape
    return pl.pallas_call(
        _row_scale_kernel,
        grid=(t // tm,),
        in_specs=[pl.BlockSpec((tm, d), lambda i: (i, 0))],
        out_specs=pl.BlockSpec((tm, 1), lambda i: (i, 0)),
        out_shape=jax.ShapeDtypeStruct((t, 1), F32),
        compiler_params=_params(("parallel",), 32),
        name="row_scale",
    )(x2)


def _s5_kernel(x_ref, r_ref, nw_ref, dsk_ref, taps_ref, win_ref, wout_ref, lpow_ref, g_ref, toep_scr, u_scr,
               *, n_chunks):
    L = taps_ref.shape[1]
    rows = u_scr.shape[0]

    @pl.when(pl.program_id(1) == 0)
    def _():
        zero = jnp.zeros((LANES, LANES), BF16)
        for s in range(L):
            for tau in range(L):
                toep_scr[s * LANES:(s + 1) * LANES, tau * LANES:(tau + 1) * LANES] = (
                    taps_ref[0, tau - s] if tau >= s else zero)

    def h_step(s):
        return x_ref[pl.ds(s, rows, stride=L), :] * r_ref[:, s:s + 1] * nw_ref[...]

    for s in range(L):
        u_scr[:, s * LANES:(s + 1) * LANES] = h_step(s).astype(BF16)
    u = u_scr[...]
    v = jnp.dot(u, win_ref[0], preferred_element_type=F32)
    half = v.shape[1] // 2
    assert n_chunks & (n_chunks - 1) == 0 and rows % n_chunks == 0
    cidx = lax.broadcasted_iota(I32, (rows, 1), 0) & (n_chunks - 1)

    def shifted(a, d):
        return jnp.where(cidx >= d, pltpu.roll(a, d, 0), 0.0)

    xr = shifted(v[:, :half], 1)
    xi = shifted(v[:, half:], 1)
    d, k = 1, 0
    while d < n_chunks:
        ar = lpow_ref[0, k:k + 1, :half]
        ai = lpow_ref[0, k:k + 1, half:]
        sr = shifted(xr, d)
        si = shifted(xi, d)
        xr, xi = xr + (ar * sr - ai * si), xi + (ar * si + ai * sr)
        d, k = 2 * d, k + 1
    state = jnp.concatenate([xr, xi], axis=1).astype(BF16)
    y = (jnp.dot(u, toep_scr[...], preferred_element_type=F32)
         + jnp.dot(state, wout_ref[0], preferred_element_type=F32))
    for tau in range(L):
        g_ref[pl.ds(tau, rows, stride=L), :] = _gelu_tanh(
            y[:, tau * LANES:(tau + 1) * LANES] + dsk_ref[...] * h_step(tau))


def _s5_operators(lam_re, lam_im, log_dt, b_re, b_im, c_re, c_im, n_chunks):
    g, p, cg = b_re.shape
    L = S5_CHUNK
    gt = LANES // cg
    nt = g // gt
    lam = lax.complex(lam_re.astype(F32), lam_im.astype(F32))
    dt = jnp.exp(log_dt.astype(F32))[:, None]
    z = lam * dt
    lam_bar = jnp.exp(z)
    b_bar = ((lam_bar - 1.0) / lam)[..., None] * lax.complex(b_re.astype(F32), b_im.astype(F32))
    cc = lax.complex(c_re.astype(F32), c_im.astype(F32))
    steps = jnp.arange(L + 1, dtype=F32)
    pw = jnp.exp(z[:, None, :] * steps[None, :, None])
    eye = jnp.eye(gt, dtype=F32)

    win = pw[:, L - 1 - jnp.arange(L), :][:, :, None, :] * jnp.transpose(b_bar, (0, 2, 1))[:, None, :, :]
    win = jnp.stack([jnp.real(win), jnp.imag(win)], axis=3).reshape(nt, gt, L, cg, 2, p)
    win_t = jnp.einsum('iascrp,ab->isacrbp', win, eye).reshape(nt, L * LANES, 2 * gt * p)
    wout = (jnp.transpose(pw[:, 1:L + 1, :], (0, 2, 1))[:, :, :, None]
            * jnp.transpose(cc, (0, 2, 1))[:, :, None, :])
    wout = jnp.stack([jnp.real(wout), -jnp.imag(wout)], axis=1).reshape(nt, gt, 2, p, L, cg)
    wout_t = jnp.einsum('iarptc,ab->iraptbc', wout, eye).reshape(nt, 2 * gt * p, L * LANES)
    m1 = cc[:, None, :, :] * pw[:, :L, None, :]
    kk = (jnp.einsum('gkop,gpi->gkio', jnp.real(m1), jnp.real(b_bar), precision=HIGHEST)
          - jnp.einsum('gkop,gpi->gkio', jnp.imag(m1), jnp.imag(b_bar), precision=HIGHEST))
    taps = jnp.einsum('iakco,ab->ikacbo', kk.reshape(nt, gt, L, cg, cg), eye).reshape(nt, L, LANES, LANES)

    n_steps = max(1, (n_chunks - 1).bit_length())
    hops = (2.0 ** jnp.arange(n_steps, dtype=F32)) * L
    lp = jnp.exp(z[:, None, :] * hops[None, :, None])
    lp = jnp.transpose(lp.reshape(nt, gt, n_steps, p), (0, 2, 1, 3)).reshape(nt, n_steps, gt * p)
    lpow = jnp.concatenate([jnp.real(lp), jnp.imag(lp)], axis=-1)
    return taps.astype(BF16), win_t.astype(BF16), wout_t.astype(BF16), lpow


def _s5_mix(x2, r, nw, d_skip, ops, bsz, seq):
    taps, win_t, wout_t, lpow = ops
    t, d = x2.shape
    L = S5_CHUNK
    n_chunks = seq // L
    nt = d // LANES
    r2 = r.reshape(t // L, L)
    return pl.pallas_call(
        functools.partial(_s5_kernel, n_chunks=n_chunks),
        grid=(nt, bsz),
        in_specs=[pl.BlockSpec((seq, LANES), lambda i, b: (b, i)),
                  pl.BlockSpec((n_chunks, L), lambda i, b: (b, 0)),
                  pl.BlockSpec((1, LANES), lambda i, b: (0, i)),
                  pl.BlockSpec((1, LANES), lambda i, b: (0, i)),
                  pl.BlockSpec((1,) + taps.shape[1:], lambda i, b: (i, 0, 0, 0)),
                  pl.BlockSpec((1,) + win_t.shape[1:], lambda i, b: (i, 0, 0)),
                  pl.BlockSpec((1,) + wout_t.shape[1:], lambda i, b: (i, 0, 0)),
                  pl.BlockSpec((1,) + lpow.shape[1:], lambda i, b: (i, 0, 0))],
        out_specs=pl.BlockSpec((seq, LANES), lambda i, b: (b, i)),
        out_shape=jax.ShapeDtypeStruct((t, d), F32),
        scratch_shapes=[pltpu.VMEM((L * LANES, L * LANES), BF16), pltpu.VMEM((n_chunks, L * LANES), BF16)],
        compiler_params=_params(("arbitrary", "arbitrary"), 48),
        name="s5_scan",
    )(x2, r2, nw.reshape(1, d), d_skip.reshape(1, d), taps, win_t, wout_t, lpow)


def _glu_kernel(g_ref, wv_ref, wg_ref, xres_ref, o_ref, g_scr):
    @pl.when(pl.program_id(1) == 0)
    def _():
        g_scr[...] = g_ref[...].astype(BF16)

    g = g_scr[...]
    val = jnp.dot(g, wv_ref[...], preferred_element_type=F32)
    gate = jnp.dot(g, wg_ref[...], preferred_element_type=F32)
    o_ref[...] = xres_ref[...] + val * _sigmoid(gate)


def _glu(x2, g, w_glu_bf, tm=1024, tn=512):
    t, d = x2.shape
    nj = d // tn
    return pl.pallas_call(
        _glu_kernel,
        grid=(t // tm, nj),
        in_specs=[pl.BlockSpec((tm, d), lambda i, j: (i, 0)),
                  pl.BlockSpec((d, tn), lambda i, j: (0, j)),
                  pl.BlockSpec((d, tn), lambda i, j: (0, j + nj)),
                  pl.BlockSpec((tm, tn), lambda i, j: (i, j))],
        out_specs=pl.BlockSpec((tm, tn), lambda i, j: (i, j)),
        out_shape=jax.ShapeDtypeStruct((t, d), F32),
        scratch_shapes=[pltpu.VMEM((tm, d), BF16)],
        compiler_params=_params(("parallel", "arbitrary"), 48),
        name="s5_glu",
    )(g, w_glu_bf, w_glu_bf, x2)


def _inproj_kernel(x_ref, nw_ref, w_ref, z_ref, h_scr):
    @pl.when(pl.program_id(1) == 0)
    def _():
        h_scr[...] = _rms(x_ref[...], nw_ref[...]).astype(BF16)

    z_ref[...] = jnp.dot(h_scr[...], w_ref[...], preferred_element_type=F32).astype(z_ref.dtype)


def _inproj(x2, nw, w_bf, tm=1024, tn=512):
    t, d = x2.shape
    n = w_bf.shape[1]
    return pl.pallas_call(
        _inproj_kernel,
        grid=(t // tm, n // tn),
        in_specs=[pl.BlockSpec((tm, d), lambda i, j: (i, 0)),
                  pl.BlockSpec((1, d), lambda i, j: (0, 0)),
                  pl.BlockSpec((d, tn), lambda i, j: (0, j))],
        out_specs=pl.BlockSpec((tm, tn), lambda i, j: (i, j)),
        out_shape=jax.ShapeDtypeStruct((t, n), BF16),
        scratch_shapes=[pltpu.VMEM((tm, d), BF16)],
        compiler_params=_params(("parallel", "arbitrary"), 48),
        name="ml_inproj",
    )(x2, nw.reshape(1, d), w_bf)


def _gates_kernel(x_ref, nw_ref, w_ref, b_ref, g_ref, *, n_heads):
    h = _rms(x_ref[...], nw_ref[...])
    pre = jnp.dot(h, w_ref[...], preferred_element_type=F32, precision=HIGHEST) + b_ref[...]
    t = GATE_SOFTCAP * jnp.tanh(pre / GATE_SOFTCAP)
    log_sig = jnp.minimum(t, 0.0) - jnp.log(1.0 + jnp.exp(-jnp.abs(t)))
    g_ref[...] = jnp.where(lax.broadcasted_iota(I32, pre.shape, 1) >= n_heads, log_sig, t)


def _gates(x2, nw, w_gate, b_gate, tm=512):
    t, d = x2.shape
    ng = w_gate.shape[1]
    return pl.pallas_call(
        functools.partial(_gates_kernel, n_heads=ng // 2),
        grid=(t // tm,),
        in_specs=[pl.BlockSpec((tm, d), lambda i: (i, 0)),
                  pl.BlockSpec((1, d), lambda i: (0, 0)),
                  pl.BlockSpec((d, ng), lambda i: (0, 0)),
                  pl.BlockSpec((1, ng), lambda i: (0, 0))],
        out_specs=pl.BlockSpec((tm, ng), lambda i: (i, 0)),
        out_shape=jax.ShapeDtypeStruct((t, ng), F32),
        compiler_params=_params(("parallel",), 32),
        name="ml_gates",
    )(x2, nw.reshape(1, d), w_gate, b_gate.reshape(1, ng))


def _mlstm_kernel(q_ref, k_ref, v_ref, o_ref, gc_ref, gr_ref, gh_ref, out_ref, c_scr, n_scr, m_scr,
                  *, n_heads, scale):
    head = pl.program_id(1)

    @pl.when(pl.program_id(2) == 0)
    def _():
        c_scr[...] = jnp.zeros_like(c_scr)
        n_scr[...] = jnp.zeros_like(n_scr)
        m_scr[...] = jnp.zeros_like(m_scr)

    gc = gc_ref[...]
    gr = gr_ref[...]
    lane_h = lax.broadcasted_iota(I32, gc.shape, 1)
    sub_h = lax.broadcasted_iota(I32, gr.shape, 0)
    i_col = jnp.sum(jnp.where(lane_h == head, gc, 0.0), axis=1, keepdims=True)
    f_col = jnp.sum(jnp.where(lane_h == head + n_heads, gc, 0.0), axis=1, keepdims=True)
    i_row = jnp.sum(jnp.where(sub_h == head, gr, 0.0), axis=0, keepdims=True)
    f_row = jnp.sum(jnp.where(sub_h == head + n_heads, gr, 0.0), axis=0, keepdims=True)

    L = gc.shape[0]
    t_idx = lax.broadcasted_iota(I32, (L, L), 0)
    s_idx = lax.broadcasted_iota(I32, (L, L), 1)
    causal = s_idx <= t_idx
    bcum_col = jnp.sum(jnp.where(causal, f_row, 0.0), axis=1, keepdims=True)
    bcum_row = jnp.sum(jnp.where(t_idx <= s_idx, f_col, 0.0), axis=0, keepdims=True)

    m_prev = m_scr[0:1, 0:1]
    a_log = bcum_col + m_prev
    d_log = jnp.where(causal, bcum_col - bcum_row + i_row, -jnp.inf)
    m_t = jnp.maximum(a_log, jnp.max(d_log, axis=1, keepdims=True))
    w_intra = jnp.exp(d_log - m_t)
    w_inter = jnp.exp(a_log - m_t)

    q = q_ref[...]
    k = k_ref[...]
    v = v_ref[...]
    q32 = q.astype(F32)
    k32 = k.astype(F32)
    c_mat = c_scr[...]
    n_row = n_scr[...]
    sc = lax.dot_general(q, k, (((1,), (1,)), ((), ())), preferred_element_type=F32) * (w_intra * scale)
    num = (w_inter * jnp.dot(q, c_mat.astype(BF16), preferred_element_type=F32)
           + jnp.dot(sc.astype(BF16), v, preferred_element_type=F32))
    den = (w_inter * jnp.sum(q32 * n_row, axis=1, keepdims=True) + jnp.sum(sc, axis=1, keepdims=True))
    h_out = num / jnp.maximum(jnp.abs(den), jnp.exp(-m_t))

    b_last = bcum_col[L - 1:L, :]
    g_log = b_last - bcum_col + i_col
    m_new = jnp.maximum(b_last + m_prev, jnp.max(g_log, axis=0, keepdims=True))
    decay = jnp.exp(b_last + m_prev - m_new)
    wk = jnp.exp(g_log - m_new) * scale
    wv = (wk * v.astype(F32)).astype(BF16)
    kv = lax.dot_general(k, wv, (((0,), (0,)), ((), ())), preferred_element_type=F32)
    c_scr[...] = decay * c_mat + kv
    n_scr[...] = decay * n_row + jnp.sum(wk * k32, axis=0, keepdims=True)
    m_scr[...] = jnp.broadcast_to(m_new, m_scr.shape)

    hn = h_out * lax.rsqrt(jnp.mean(h_out * h_out, axis=-1, keepdims=True) + EPS) * gh_ref[...]
    out_ref[...] = (_sigmoid(o_ref[...].astype(F32)) * hn).astype(out_ref.dtype)


def _mlstm(z, gcol, grow, g_head, bsz, seq, n_heads, dk, dv):
    t = bsz * seq
    L = ML_CHUNK
    nc = seq // L
    qk_w = n_heads * dk
    ng = 2 * n_heads
    k_off = qk_w // dk
    v_off = (2 * qk_w) // dv
    o_off = (2 * qk_w + n_heads * dv) // dv
    return pl.pallas_call(
        functools.partial(_mlstm_kernel, n_heads=n_heads, scale=dk ** -0.5),
        grid=(bsz, n_heads, nc),
        in_specs=[pl.BlockSpec((L, dk), lambda b, h, c: (b * nc + c, h)),
                  pl.BlockSpec((L, dk), lambda b, h, c: (b * nc + c, k_off + h)),
                  pl.BlockSpec((L, dv), lambda b, h, c: (b * nc + c, v_off + h)),
                  pl.BlockSpec((L, dv), lambda b, h, c: (b * nc + c, o_off + h)),
                  pl.BlockSpec((L, ng), lambda b, h, c: (b * nc + c, 0)),
                  pl.BlockSpec((ng, L), lambda b, h, c: (0, b * nc + c)),
                  pl.BlockSpec((1, dv), lambda b, h, c: (0, h))],
        out_specs=pl.BlockSpec((L, dv), lambda b, h, c: (b * nc + c, h)),
        out_shape=jax.ShapeDtypeStruct((t, n_heads * dv), BF16),
        scratch_shapes=[pltpu.VMEM((dk, dv), F32), pltpu.VMEM((1, dk), F32), pltpu.VMEM((1, LANES), F32)],
        compiler_params=_params(("parallel", "parallel", "arbitrary"), 32),
        name="ml_chunk",
    )(z, z, z, z, gcol, grow, g_head.reshape(1, n_heads * dv))


def _outproj_kernel(a_ref, w_ref, xres_ref, o_ref):
    o_ref[...] = xres_ref[...] + jnp.dot(a_ref[...], w_ref[...], preferred_element_type=F32)


def _outproj(a_bf, w_bf, x2, tm=1024, tn=512):
    t, kdim = a_bf.shape
    d = w_bf.shape[1]
    return pl.pallas_call(
        _outproj_kernel,
        grid=(t // tm, d // tn),
        in_specs=[pl.BlockSpec((tm, kdim), lambda i, j: (i, 0)),
                  pl.BlockSpec((kdim, tn), lambda i, j: (0, j)),
                  pl.BlockSpec((tm, tn), lambda i, j: (i, j))],
        out_specs=pl.BlockSpec((tm, tn), lambda i, j: (i, j)),
        out_shape=jax.ShapeDtypeStruct((t, d), F32),
        compiler_params=_params(("parallel", "arbitrary"), 40),
        name="ml_outproj",
    )(a_bf, w_bf, x2)


def _route_kernel(x_ref, nw_ref, wt_ref, b_ref, h_ref, eidx_ref, gate_ref, *, n_groups, per_group):
    h = _rms(x_ref[...], nw_ref[...])
    h_ref[...] = h
    logits = lax.dot_general(wt_ref[...], h, (((1,), (1,)), ((), ())),
                             preferred_element_type=F32, precision=HIGHEST) + b_ref[...]
    n_exp = n_groups * per_group
    lg = logits[0:n_groups, :]
    le = logits[n_groups:n_groups + n_exp, :]
    g_iota = lax.broadcasted_iota(I32, lg.shape, 0)
    g_max = jnp.max(lg, axis=0, keepdims=True)
    g_sel = jnp.min(jnp.where(lg == g_max, g_iota, n_groups), axis=0, keepdims=True)
    p_sel = 1.0 / jnp.sum(jnp.exp(lg - g_max), axis=0, keepdims=True)

    e_iota = lax.broadcasted_iota(I32, le.shape, 0)
    lo = g_sel * per_group
    cand = jnp.where((e_iota >= lo) & (e_iota < lo + per_group), le, -jnp.inf)
    v1 = jnp.max(cand, axis=0, keepdims=True)
    i1 = jnp.min(jnp.where(cand == v1, e_iota, n_exp), axis=0, keepdims=True)
    cand2 = jnp.where(e_iota == i1, -jnp.inf, cand)
    v2 = jnp.max(cand2, axis=0, keepdims=True)
    i2 = jnp.min(jnp.where(cand2 == v2, e_iota, n_exp), axis=0, keepdims=True)
    e2 = jnp.exp(v2 - v1)
    inv = p_sel / (1.0 + e2)
    eidx_ref[...] = jnp.concatenate([i1, i2], axis=0)
    gate_ref[...] = jnp.concatenate([inv, e2 * inv], axis=0)


def _route(x2, nw, w_group, b_group, w_expert, b_expert, tm=512):
    t, d = x2.shape
    n_groups = w_group.shape[1]
    n_exp = w_expert.shape[1]
    n_rows = -(-(n_groups + n_exp) // LANES) * LANES
    wt = jnp.zeros((n_rows, d), F32).at[:n_groups].set(w_group.T.astype(F32))
    wt = wt.at[n_groups:n_groups + n_exp].set(w_expert.T.astype(F32))
    bias = jnp.zeros((n_rows, 1), F32).at[:n_groups, 0].set(b_group.astype(F32))
    bias = bias.at[n_groups:n_groups + n_exp, 0].set(b_expert.astype(F32))
    return pl.pallas_call(
        functools.partial(_route_kernel, n_groups=n_groups, per_group=n_exp // n_groups),
        grid=(t // tm,),
        in_specs=[pl.BlockSpec((tm, d), lambda i: (i, 0)),
                  pl.BlockSpec((1, d), lambda i: (0, 0)),
                  pl.BlockSpec((n_rows, d), lambda i: (0, 0)),
                  pl.BlockSpec((n_rows, 1), lambda i: (0, 0))],
        out_specs=[pl.BlockSpec((tm, d), lambda i: (i, 0)),
                   pl.BlockSpec((MOE_TOPK, tm), lambda i: (0, i)),
                   pl.BlockSpec((MOE_TOPK, tm), lambda i: (0, i))],
        out_shape=[jax.ShapeDtypeStruct((t, d), F32),
                   jax.ShapeDtypeStruct((MOE_TOPK, t), I32),
                   jax.ShapeDtypeStruct((MOE_TOPK, t), F32)],
        compiler_params=_params(("parallel",), 40),
        name="moe_route",
    )(x2, nw.reshape(1, d), wt, bias)


def _meta_kernel(eidx_ref, dest_ref, first_ref, nblk_ref, *, n_exp, n_tiles):
    tl = eidx_ref.shape[1]
    e_iota = lax.broadcasted_iota(I32, (n_exp, tl), 0)
    upper = (lax.broadcasted_iota(I32, (tl, tl), 0) < lax.broadcasted_iota(I32, (tl, tl), 1)).astype(BF16)

    def hits(i):
        return [e_iota == eidx_ref[pl.ds(k * n_tiles + i, 1), :] for k in range(MOE_TOPK)]

    def rank_tile(i, carry):
        hit = hits(i)
        member = sum(h.astype(F32) for h in hit)
        before = jnp.dot(member.astype(BF16), upper, preferred_element_type=F32) + carry
        for k in range(MOE_TOPK):
            rank = jnp.sum(jnp.where(hit[k], before, 0.0), axis=0, keepdims=True)
            dest_ref[pl.ds(k * n_tiles + i, 1), :] = rank.astype(I32)
        return carry + jnp.sum(member, axis=1, keepdims=True)

    counts = lax.fori_loop(0, n_tiles, rank_tile, jnp.zeros((n_exp, 1), F32)).astype(I32)
    nblk = (counts + (MOE_ROWS - 1)) >> MOE_ROWS_LOG2
    lower = (lax.broadcasted_iota(I32, (n_exp, n_exp), 1) < lax.broadcasted_iota(I32, (n_exp, n_exp), 0))
    first = jnp.dot(lower.astype(F32), jnp.broadcast_to(nblk.astype(F32), (n_exp, LANES)),
                    preferred_element_type=F32, precision=HIGHEST)[:, 0:1].astype(I32)
    pad_start = first << MOE_ROWS_LOG2

    def dest_tile(i, carry):
        hit = hits(i)
        for k in range(MOE_TOPK):
            base = jnp.sum(jnp.where(hit[k], pad_start, 0), axis=0, keepdims=True)
            row = pl.ds(k * n_tiles + i, 1)
            dest_ref[row, :] = dest_ref[row, :] + base
        return carry

    lax.fori_loop(0, n_tiles, dest_tile, 0)
    first_ref[...] = first
    nblk_ref[...] = nblk


def _meta(eidx, n_exp, tl=512):
    n_tok = eidx.shape[1]
    n_tiles = n_tok // tl
    dest, first, nblk = pl.pallas_call(
        functools.partial(_meta_kernel, n_exp=n_exp, n_tiles=n_tiles),
        out_shape=[jax.ShapeDtypeStruct((MOE_TOPK * n_tiles, tl), I32),
                   jax.ShapeDtypeStruct((n_exp, 1), I32),
                   jax.ShapeDtypeStruct((n_exp, 1), I32)],
        compiler_params=pltpu.CompilerParams(vmem_limit_bytes=32 * 1024 * 1024),
        name="moe_meta",
    )(eidx.reshape(MOE_TOPK * n_tiles, tl))
    return dest.reshape(MOE_TOPK, n_tok), first[:, 0], nblk[:, 0]


def _row_copy(src_ref, src_row, dst_ref, dst_row, sem):
    return pltpu.make_async_copy(src_ref.at[pl.ds(src_row, 1), :], dst_ref.at[pl.ds(dst_row, 1), :], sem)


def _dispatch_kernel(d0_ref, d1_ref, first_ref, nblk_ref, h_ref, xs_ref, zero_scr, sem, *, tt, n_exp):
    base = pl.program_id(0) * tt
    dests = (d0_ref, d1_ref)

    def pad_copy(e):
        last = pl.multiple_of((first_ref[e] + nblk_ref[e] - 1) * MOE_ROWS, MOE_ROWS)
        return pltpu.make_async_copy(zero_scr, xs_ref.at[pl.ds(last, MOE_ROWS), :], sem)

    def tail_copy(b):
        return pltpu.make_async_copy(
            zero_scr, xs_ref.at[pl.ds(pl.multiple_of(b * MOE_ROWS, MOE_ROWS), MOE_ROWS), :], sem)

    @pl.when(pl.program_id(0) == 0)
    def _():
        zero_scr[...] = jnp.zeros_like(zero_scr)
        for e in range(n_exp):
            @pl.when(nblk_ref[e] > 0)
            def _():
                pad_copy(e).start()
        for e in range(n_exp):
            @pl.when(nblk_ref[e] > 0)
            def _():
                pad_copy(e).wait()
        n_used = first_ref[n_exp - 1] + nblk_ref[n_exp - 1]
        n_blocks = xs_ref.shape[0] // MOE_ROWS
        lax.fori_loop(n_used, n_blocks, lambda b, c: (tail_copy(b).start(), c)[1], 0)
        lax.fori_loop(n_used, n_blocks, lambda b, c: (tail_copy(b).wait(), c)[1], 0)

    def start(t, c):
        for dref in dests:
            _row_copy(h_ref, t, xs_ref, dref[base + t], sem).start()
        return c

    def wait(t, c):
        for dref in dests:
            _row_copy(h_ref, t, xs_ref, dref[base + t], sem).wait()
        return c

    lax.fori_loop(0, tt, start, 0, unroll=DMA_UNROLL)
    lax.fori_loop(0, tt, wait, 0, unroll=DMA_UNROLL)


def _dispatch(h, dest, first, nblk, n_slots, tt=1024):
    n_tok, d = h.shape
    return pl.pallas_call(
        functools.partial(_dispatch_kernel, tt=tt, n_exp=first.shape[0]),
        grid_spec=pltpu.PrefetchScalarGridSpec(
            num_scalar_prefetch=4,
            grid=(n_tok // tt,),
            in_specs=[pl.BlockSpec((tt, d), lambda i, *_: (i, 0))],
            out_specs=pl.BlockSpec(memory_space=pl.ANY),
            scratch_shapes=[pltpu.VMEM((MOE_ROWS, d), h.dtype), pltpu.SemaphoreType.DMA(())]),
        out_shape=jax.ShapeDtypeStruct((n_slots, d), h.dtype),
        compiler_params=_params(("arbitrary",), 40),
        name="moe_dispatch",
    )(dest[0], dest[1], first, nblk, h)


def _expert_kernel(first_ref, nblk_ref, xs_ref, wgu_hbm, wdn_hbm, ys_ref, wgu_stage, wdn_stage, wgu_scr, wdn_scr,
                   xbuf, ybuf, sem_w, sem_x, sem_y, *, layer, hidden):
    e = pl.program_id(0)
    n_exp = pl.num_programs(0)
    nb = nblk_ref[e]
    first = first_ref[e]

    def w_copies(ex):
        return (pltpu.make_async_copy(wgu_hbm.at[layer, ex], wgu_stage, sem_w.at[0]),
                pltpu.make_async_copy(wdn_hbm.at[layer, ex], wdn_stage, sem_w.at[1]))

    def x_copy(b, slot):
        rows = pl.ds(pl.multiple_of((first + b) * MOE_ROWS, MOE_ROWS), MOE_ROWS)
        return pltpu.make_async_copy(xs_ref.at[rows, :], xbuf.at[slot], sem_x.at[slot])

    def y_copy(b, slot):
        rows = pl.ds(pl.multiple_of((first + b) * MOE_ROWS, MOE_ROWS), MOE_ROWS)
        return pltpu.make_async_copy(ybuf.at[slot], ys_ref.at[rows, :], sem_y.at[slot])

    @pl.when(e == 0)
    def _():
        for c in w_copies(0):
            c.start()

    @pl.when(nb > 0)
    def _():
        x_copy(0, 0).start()

    for c in w_copies(e):
        c.wait()
    wgu_scr[...] = wgu_stage[...].astype(BF16)
    wdn_scr[...] = wdn_stage[...].astype(BF16)

    @pl.when(e + 1 < n_exp)
    def _():
        for c in w_copies(e + 1):
            c.start()

    def block(b, carry):
        slot = b & 1
        x_copy(b, slot).wait()

        @pl.when(b + 1 < nb)
        def _():
            x_copy(b + 1, 1 - slot).start()

        @pl.when(b >= 2)
        def _():
            y_copy(b - 2, slot).wait()

        gu = jnp.dot(xbuf[slot].astype(BF16), wgu_scr[...], preferred_element_type=F32)
        g = gu[:, :hidden]
        act = (g * _sigmoid(g)) * gu[:, hidden:]
        ybuf[slot] = jnp.dot(act.astype(BF16), wdn_scr[...], preferred_element_type=F32)
        y_copy(b, slot).start()
        return carry

    lax.fori_loop(0, nb, block, 0)

    @pl.when(nb >= 2)
    def _():
        y_copy(nb - 2, nb & 1).wait()

    @pl.when(nb >= 1)
    def _():
        y_copy(nb - 1, (nb - 1) & 1).wait()

    @pl.when(e == n_exp - 1)
    def _():
        ybuf[0] = jnp.zeros(ybuf.shape[1:], ybuf.dtype)
        n_blocks = ys_ref.shape[0] // MOE_ROWS
        lax.fori_loop(nb, n_blocks - first, lambda b, c: (y_copy(b, 0).start(), c)[1], 0)
        lax.fori_loop(nb, n_blocks - first, lambda b, c: (y_copy(b, 0).wait(), c)[1], 0)


def _experts(xs, first, nblk, layer, w_gu, w_down):
    n_exp, d, two_f = w_gu.shape[1], w_gu.shape[2], w_gu.shape[3]
    hidden = two_f // 2
    return pl.pallas_call(
        functools.partial(_expert_kernel, layer=layer, hidden=hidden),
        grid_spec=pltpu.PrefetchScalarGridSpec(
            num_scalar_prefetch=2,
            grid=(n_exp,),
            in_specs=[pl.BlockSpec(memory_space=pl.ANY),
                      pl.BlockSpec(memory_space=pl.ANY),
                      pl.BlockSpec(memory_space=pl.ANY)],
            out_specs=pl.BlockSpec(memory_space=pl.ANY),
            scratch_shapes=[pltpu.VMEM((d, two_f), w_gu.dtype), pltpu.VMEM((hidden, d), w_down.dtype),
                            pltpu.VMEM((d, two_f), BF16), pltpu.VMEM((hidden, d), BF16),
                            pltpu.VMEM((2, MOE_ROWS, d), xs.dtype), pltpu.VMEM((2, MOE_ROWS, d), F32),
                            pltpu.SemaphoreType.DMA((2,)), pltpu.SemaphoreType.DMA((2,)),
                            pltpu.SemaphoreType.DMA((2,))]),
        out_shape=jax.ShapeDtypeStruct(xs.shape, F32),
        compiler_params=_params(("arbitrary",), 52),
        name="moe_experts",
    )(first, nblk, xs, w_gu, w_down)


def _combine_kernel(d0_ref, d1_ref, x_ref, g_ref, nw_ref, ys_ref, o_ref, buf, sem, *, tt, final_norm):
    base = pl.program_id(0) * tt
    dests = (d0_ref, d1_ref)

    def start(t, c):
        for k, dref in enumerate(dests):
            _row_copy(ys_ref, dref[base + t], buf.at[k], t, sem).start()
        return c

    def wait(t, c):
        for k, dref in enumerate(dests):
            _row_copy(ys_ref, dref[base + t], buf.at[k], t, sem).wait()
        return c

    lax.fori_loop(0, tt, start, 0, unroll=DMA_UNROLL)
    lax.fori_loop(0, tt, wait, 0, unroll=DMA_UNROLL)
    g = g_ref[...]
    out = x_ref[...] + sum(g[:, k:k + 1] * buf[k] for k in range(MOE_TOPK))
    o_ref[...] = _rms(out, nw_ref[...]) if final_norm else out


def _combine(x2, ys, dest, gates_t, final_w, tt=256):
    t, d = x2.shape
    nw = jnp.ones((1, d), F32) if final_w is None else final_w.reshape(1, d).astype(F32)
    return pl.pallas_call(
        functools.partial(_combine_kernel, tt=tt, final_norm=final_w is not None),
        grid_spec=pltpu.PrefetchScalarGridSpec(
            num_scalar_prefetch=2,
            grid=(t // tt,),
            in_specs=[pl.BlockSpec((tt, d), lambda i, d0, d1: (i, 0)),
                      pl.BlockSpec((tt, MOE_TOPK), lambda i, d0, d1: (i, 0)),
                      pl.BlockSpec((1, d), lambda i, d0, d1: (0, 0)),
                      pl.BlockSpec(memory_space=pl.ANY)],
            out_specs=pl.BlockSpec((tt, d), lambda i, d0, d1: (i, 0)),
            scratch_shapes=[pltpu.VMEM((MOE_TOPK, tt, d), F32), pltpu.SemaphoreType.DMA(())]),
        out_shape=jax.ShapeDtypeStruct((t, d), F32),
        compiler_params=_params(("arbitrary",), 40),
        name="moe_combine",
    )(dest[0], dest[1], x2, gates_t, nw, ys)


def _hier_moe(x2, layer, nw, w_group, b_group, w_expert, b_expert, w_gu, w_down, final_w):
    t = x2.shape[0]
    n_exp = w_expert.shape[1]
    n_blocks = (t * MOE_TOPK) // MOE_ROWS + n_exp
    h, eidx, gates = _route(x2, nw, w_group, b_group, w_expert, b_expert)
    dest, first, nblk = _meta(eidx, n_exp)
    xs = _dispatch(h, dest, first, nblk, n_blocks * MOE_ROWS)
    ys = _experts(xs, first, nblk, layer, w_gu, w_down)
    return _combine(x2, ys, dest, gates.T, final_w)


def kernel(x, s5_lam_re, s5_lam_im, s5_log_dt, s5_b_re, s5_b_im, s5_c_re, s5_c_im, s5_d, s5_w_glu, ml_w_in, ml_b_gate, ml_g_head, ml_w_out, norm_mix, norm_ffn, moe_w_group, moe_b_group, moe_w_expert, moe_b_expert, moe_w_gu, moe_w_down, norm_final):
    bsz, seq, d = x.shape
    assert norm_mix.shape[0] == 2 and s5_w_glu.shape[0] == 1 and ml_w_in.shape[0] == 1
    x2 = x.reshape(bsz * seq, d).astype(F32)

    ops = _s5_operators(s5_lam_re[0], s5_lam_im[0], s5_log_dt[0], s5_b_re[0], s5_b_im[0],
                        s5_c_re[0], s5_c_im[0], seq // S5_CHUNK)
    g = _s5_mix(x2, _row_scale(x2), norm_mix[0].astype(F32), s5_d[0].astype(F32), ops, bsz, seq)
    x2 = _glu(x2, g, s5_w_glu[0].astype(BF16))
    x2 = _hier_moe(x2, 0, norm_ffn[0], moe_w_group[0], moe_b_group[0], moe_w_expert[0], moe_b_expert[0],
                   moe_w_gu, moe_w_down, None)

    n_heads = ml_b_gate.shape[1] // 2
    dv = ml_w_out.shape[1] // n_heads
    dk = (ml_w_in.shape[2] - 2 * n_heads * dv - 2 * n_heads) // (2 * n_heads)
    n_main = 2 * n_heads * (dk + dv)
    z = _inproj(x2, norm_mix[1], ml_w_in[0, :, :n_main].astype(BF16))
    gcol = _gates(x2, norm_mix[1], ml_w_in[0, :, n_main:].astype(F32), ml_b_gate[0].astype(F32))
    a_bf = _mlstm(z, gcol, gcol.T, ml_g_head[0].astype(F32), bsz, seq, n_heads, dk, dv)
    x2 = _outproj(a_bf, ml_w_out[0].astype(BF16), x2)
    x2 = _hier_moe(x2, 1, norm_ffn[1], moe_w_group[1], moe_b_group[1], moe_w_expert[1], moe_b_expert[1],
                   moe_w_gu, moe_w_down, norm_final)
    return x2.reshape(bsz, seq, d).astype(x.dtype)
```

```python
import functools
import math

import jax
import jax.numpy as jnp
from jax import lax
from jax.experimental import pallas as pl
from jax.experimental.pallas import tpu as pltpu

F32 = jnp.float32
BF16 = jnp.bfloat16
I32 = jnp.int32

EPS = 1e-6
GATE_SOFTCAP = 15.0
MOE_TOPK = 2

LANES = 128
S5_CHUNK = 16
ML_CHUNK = 256
MOE_ROWS_LOG2 = 8
MOE_ROWS = 1 << MOE_ROWS_LOG2
VMEM_BYTES = 64 * 1024 * 1024
DMA_UNROLL = 8
W_DMA_CHUNKS = 4
HIGHEST = lax.Precision.HIGHEST


def _params(sem, vmem_mib):
    assert vmem_mib * 1024 * 1024 < VMEM_BYTES
    return pltpu.CompilerParams(dimension_semantics=sem, vmem_limit_bytes=vmem_mib * 1024 * 1024)


def _rms(x32, w):
    ms = jnp.mean(x32 * x32, axis=-1, keepdims=True)
    return x32 * lax.rsqrt(ms + EPS) * w


def _sigmoid(x):
    return 1.0 / (1.0 + jnp.exp(-x))


def _gelu_tanh(x):
    c = math.sqrt(2.0 / math.pi)
    return 0.5 * x * (1.0 + jnp.tanh(c * (x + 0.044715 * (x * x * x))))


HI16 = -65536
LO16 = 65535


def _pack_bf16(v):
    half = v.shape[1] // 2
    bits = lax.bitcast_convert_type(v.astype(BF16).astype(F32), I32)
    return (bits[:, half:] & HI16) | ((bits[:, :half] >> 16) & LO16)


def _unpack_bf16(w):
    lo = lax.bitcast_convert_type(w << 16, F32)
    hi = lax.bitcast_convert_type(w & HI16, F32)
    return jnp.concatenate([lo, hi], axis=1)


def _row_scale_kernel(x_ref, r_ref):
    x = x_ref[...]
    r_ref[...] = lax.rsqrt(jnp.mean(x * x, axis=-1, keepdims=True) + EPS)


def _row_scale(x2, tm=512):
    t, d = x2.shape
    return pl.pallas_call(
        _row_scale_kernel,
        grid=(t // tm,),
        in_specs=[pl.BlockSpec((tm, d), lambda i: (i, 0))],
        out_specs=pl.BlockSpec((tm, 1), lambda i: (i, 0)),
        out_shape=jax.ShapeDtypeStruct((t, 1), F32),
        compiler_params=_params(("parallel",), 32),
        name="row_scale",
    )(x2)


def _s5_build_operators(bexp_ref, cexp_ref, qpow_ref, ppow_ref, win_scr, wout_scr, toep_scr):
    L = qpow_ref.shape[1]
    half = bexp_ref.shape[2] // 2

    def tile(ref, s, t):
        return ref.at[s * LANES:(s + 1) * LANES, t * LANES:(t + 1) * LANES]

    br, bi = bexp_ref[0, :, :half], bexp_ref[0, :, half:]
    cr, ci = cexp_ref[0, :half, :], cexp_ref[0, half:, :]
    b_cat = bexp_ref[0].astype(BF16)
    for s in range(L):
        qr, qi = qpow_ref[0, s:s + 1, :half], qpow_ref[0, s:s + 1, half:]
        win_scr[s * LANES:(s + 1) * LANES, :] = jnp.concatenate(
            [br * qr - bi * qi, br * qi + bi * qr], axis=1).astype(BF16)
    for t in range(L + 1):
        pr, pi = ppow_ref[0, :half, t:t + 1], ppow_ref[0, half:, t:t + 1]
        w_t = jnp.concatenate([cr * pr - ci * pi, -(cr * pi + ci * pr)], axis=0).astype(BF16)
        if t >= 1:
            wout_scr[:, (t - 1) * LANES:t * LANES] = w_t
        if t < L:
            tap = jnp.dot(b_cat, w_t, preferred_element_type=F32).astype(BF16)
            for s in range(L - t):
                tile(toep_scr, s, s + t)[...] = tap
    zero = jnp.zeros((LANES, LANES), BF16)
    for s in range(L):
        for tau in range(s):
            tile(toep_scr, s, tau)[...] = zero


def _s5_kernel(x_ref, r_ref, nw_ref, dsk_ref, bexp_ref, cexp_ref, qpow_ref, ppow_ref, lpow_ref, g_ref,
               win_scr, wout_scr, toep_scr, u_scr, *, n_chunks):
    L = qpow_ref.shape[1]
    rows = u_scr.shape[0]

    @pl.when(pl.program_id(1) == 0)
    def _():
        _s5_build_operators(bexp_ref, cexp_ref, qpow_ref, ppow_ref, win_scr, wout_scr, toep_scr)

    def h_step(s):
        return x_ref[pl.ds(s, rows, stride=L), :] * r_ref[:, s:s + 1] * nw_ref[...]

    for s in range(L):
        u_scr[:, s * LANES:(s + 1) * LANES] = h_step(s).astype(BF16)
    u = u_scr[...]
    v = jnp.dot(u, win_scr[...], preferred_element_type=F32)
    half = v.shape[1] // 2
    assert n_chunks & (n_chunks - 1) == 0 and rows % n_chunks == 0
    cidx = lax.broadcasted_iota(I32, (rows, 1), 0) & (n_chunks - 1)

    def shifted(a, d):
        return jnp.where(cidx >= d, pltpu.roll(a, d, 0), 0.0)

    xr = shifted(v[:, :half], 1)
    xi = shifted(v[:, half:], 1)
    d, k = 1, 0
    while d < n_chunks:
        ar = lpow_ref[0, k:k + 1, :half]
        ai = lpow_ref[0, k:k + 1, half:]
        sr = shifted(xr, d)
        si = shifted(xi, d)
        xr, xi = xr + (ar * sr - ai * si), xi + (ar * si + ai * sr)
        d, k = 2 * d, k + 1
    state = jnp.concatenate([xr, xi], axis=1).astype(BF16)
    y = (jnp.dot(u, toep_scr[...], preferred_element_type=F32)
         + jnp.dot(state, wout_scr[...], preferred_element_type=F32))
    for tau in range(L):
        g_ref[pl.ds(tau, rows, stride=L), :] = _gelu_tanh(
            y[:, tau * LANES:(tau + 1) * LANES] + dsk_ref[...] * h_step(tau))


def _s5_operators(lam_re, lam_im, log_dt, b_re, b_im, c_re, c_im, n_chunks):
    g, p, cg = b_re.shape
    L = S5_CHUNK
    gt = LANES // cg
    nt = g // gt
    lam = lax.complex(lam_re.astype(F32), lam_im.astype(F32))
    dt = jnp.exp(log_dt.astype(F32))[:, None]
    z = lam * dt
    lam_bar = jnp.exp(z)
    b_bar = ((lam_bar - 1.0) / lam)[..., None] * lax.complex(b_re.astype(F32), b_im.astype(F32))
    cc = lax.complex(c_re.astype(F32), c_im.astype(F32))
    steps = jnp.arange(L + 1, dtype=F32)
    pw = jnp.exp(z[:, None, :] * steps[None, :, None])
    eye = jnp.eye(gt, dtype=F32)

    def re_im(a, axis):
        return jnp.stack([jnp.real(a), jnp.imag(a)], axis=axis)

    def per_tile(a):
        k = a.shape[1]
        return jnp.transpose(a.reshape(nt, gt, k, p), (0, 2, 1, 3)).reshape(nt, k, gt * p)

    bexp = re_im(jnp.transpose(b_bar, (0, 2, 1)), 2).reshape(nt, gt, cg, 2, p)
    bexp = jnp.einsum('iacrp,ab->iacrbp', bexp, eye).reshape(nt, gt * cg, 2 * gt * p)
    cexp = re_im(jnp.transpose(cc, (0, 2, 1)), 1).reshape(nt, gt, 2, p, cg)
    cexp = jnp.einsum('iarpc,ab->irapbc', cexp, eye).reshape(nt, 2 * gt * p, gt * cg)
    q = pw[:, L - 1 - jnp.arange(L), :]
    qpow = jnp.concatenate([per_tile(jnp.real(q)), per_tile(jnp.imag(q))], axis=-1)
    pt = jnp.concatenate([per_tile(jnp.real(pw)), per_tile(jnp.imag(pw))], axis=-1)
    ppow = jnp.zeros((nt, 2 * gt * p, LANES), F32).at[:, :, :L + 1].set(jnp.transpose(pt, (0, 2, 1)))

    n_steps = max(1, (n_chunks - 1).bit_length())
    hops = (2.0 ** jnp.arange(n_steps, dtype=F32)) * L
    lp = jnp.exp(z[:, None, :] * hops[None, :, None])
    lpow = jnp.concatenate([per_tile(jnp.real(lp)), per_tile(jnp.imag(lp))], axis=-1)
    return bexp, cexp, qpow, ppow, lpow


def _s5_mix(x2, r, nw, d_skip, ops, bsz, seq):
    t, d = x2.shape
    L = S5_CHUNK
    n_chunks = seq // L
    nt = d // LANES
    n_state = ops[0].shape[2]
    r2 = r.reshape(t // L, L)
    return pl.pallas_call(
        functools.partial(_s5_kernel, n_chunks=n_chunks),
        grid=(nt, bsz),
        in_specs=[pl.BlockSpec((seq, LANES), lambda i, b: (b, i)),
                  pl.BlockSpec((n_chunks, L), lambda i, b: (b, 0)),
                  pl.BlockSpec((1, LANES), lambda i, b: (0, i)),
                  pl.BlockSpec((1, LANES), lambda i, b: (0, i))]
                 + [pl.BlockSpec((1,) + a.shape[1:], lambda i, b: (i, 0, 0)) for a in ops],
        out_specs=pl.BlockSpec((seq, LANES), lambda i, b: (b, i)),
        out_shape=jax.ShapeDtypeStruct((t, d), F32),
        scratch_shapes=[pltpu.VMEM((L * LANES, n_state), BF16), pltpu.VMEM((n_state, L * LANES), BF16),
                        pltpu.VMEM((L * LANES, L * LANES), BF16), pltpu.VMEM((n_chunks, L * LANES), BF16)],
        compiler_params=_params(("arbitrary", "arbitrary"), 48),
        name="s5_scan",
    )(x2, r2, nw.reshape(1, d), d_skip.reshape(1, d), *ops)


def _glu_kernel(g_ref, wv_ref, wg_ref, xres_ref, o_ref, g_scr):
    @pl.when(pl.program_id(1) == 0)
    def _():
        g_scr[...] = g_ref[...].astype(BF16)

    g = g_scr[...]
    val = jnp.dot(g, wv_ref[...], preferred_element_type=F32)
    gate = jnp.dot(g, wg_ref[...], preferred_element_type=F32)
    o_ref[...] = xres_ref[...] + val * _sigmoid(gate)


def _glu(x2, g, w_glu_bf, tm=1024, tn=512):
    t, d = x2.shape
    nj = d // tn
    return pl.pallas_call(
        _glu_kernel,
        grid=(t // tm, nj),
        in_specs=[pl.BlockSpec((tm, d), lambda i, j: (i, 0)),
                  pl.BlockSpec((d, tn), lambda i, j: (0, j)),
                  pl.BlockSpec((d, tn), lambda i, j: (0, j + nj)),
                  pl.BlockSpec((tm, tn), lambda i, j: (i, j))],
        out_specs=pl.BlockSpec((tm, tn), lambda i, j: (i, j)),
        out_shape=jax.ShapeDtypeStruct((t, d), F32),
        scratch_shapes=[pltpu.VMEM((tm, d), BF16)],
        compiler_params=_params(("parallel", "arbitrary"), 48),
        name="s5_glu",
    )(g, w_glu_bf, w_glu_bf, x2)


def _inproj_kernel(x_ref, nw_ref, w_ref, z_ref, h_scr):
    @pl.when(pl.program_id(1) == 0)
    def _():
        h_scr[...] = _rms(x_ref[...], nw_ref[...]).astype(BF16)

    z_ref[...] = jnp.dot(h_scr[...], w_ref[...], preferred_element_type=F32).astype(z_ref.dtype)


def _inproj(x2, nw, w_bf, tm=1024, tn=512):
    t, d = x2.shape
    n = w_bf.shape[1]
    return pl.pallas_call(
        _inproj_kernel,
        grid=(t // tm, n // tn),
        in_specs=[pl.BlockSpec((tm, d), lambda i, j: (i, 0)),
                  pl.BlockSpec((1, d), lambda i, j: (0, 0)),
                  pl.BlockSpec((d, tn), lambda i, j: (0, j))],
        out_specs=pl.BlockSpec((tm, tn), lambda i, j: (i, j)),
        out_shape=jax.ShapeDtypeStruct((t, n), BF16),
        scratch_shapes=[pltpu.VMEM((tm, d), BF16)],
        compiler_params=_params(("parallel", "arbitrary"), 48),
        name="ml_inproj",
    )(x2, nw.reshape(1, d), w_bf)


def _gates_kernel(x_ref, nw_ref, w_ref, b_ref, g_ref, *, n_heads):
    h = _rms(x_ref[...], nw_ref[...])
    pre = jnp.dot(h, w_ref[...], preferred_element_type=F32, precision=HIGHEST) + b_ref[...]
    t = GATE_SOFTCAP * jnp.tanh(pre / GATE_SOFTCAP)
    log_sig = jnp.minimum(t, 0.0) - jnp.log(1.0 + jnp.exp(-jnp.abs(t)))
    g_ref[...] = jnp.where(lax.broadcasted_iota(I32, pre.shape, 1) >= n_heads, log_sig, t)


def _gates(x2, nw, w_gate, b_gate, tm=512):
    t, d = x2.shape
    ng = w_gate.shape[1]
    return pl.pallas_call(
        functools.partial(_gates_kernel, n_heads=ng // 2),
        grid=(t // tm,),
        in_specs=[pl.BlockSpec((tm, d), lambda i: (i, 0)),
                  pl.BlockSpec((1, d), lambda i: (0, 0)),
                  pl.BlockSpec((d, ng), lambda i: (0, 0)),
                  pl.BlockSpec((1, ng), lambda i: (0, 0))],
        out_specs=pl.BlockSpec((tm, ng), lambda i: (i, 0)),
        out_shape=jax.ShapeDtypeStruct((t, ng), F32),
        compiler_params=_params(("parallel",), 32),
        name="ml_gates",
    )(x2, nw.reshape(1, d), w_gate, b_gate.reshape(1, ng))


def _mlstm_kernel(q_ref, k_ref, v_ref, o_ref, gc_ref, gr_ref, gh_ref, out_ref, c_scr, n_scr, m_scr,
                  *, n_heads, scale):
    head = pl.program_id(1)

    @pl.when(pl.program_id(2) == 0)
    def _():
        c_scr[...] = jnp.zeros_like(c_scr)
        n_scr[...] = jnp.zeros_like(n_scr)
        m_scr[...] = jnp.zeros_like(m_scr)

    gc = gc_ref[...]
    gr = gr_ref[...]
    lane_h = lax.broadcasted_iota(I32, gc.shape, 1)
    sub_h = lax.broadcasted_iota(I32, gr.shape, 0)
    i_col = jnp.sum(jnp.where(lane_h == head, gc, 0.0), axis=1, keepdims=True)
    f_col = jnp.sum(jnp.where(lane_h == head + n_heads, gc, 0.0), axis=1, keepdims=True)
    i_row = jnp.sum(jnp.where(sub_h == head, gr, 0.0), axis=0, keepdims=True)
    f_row = jnp.sum(jnp.where(sub_h == head + n_heads, gr, 0.0), axis=0, keepdims=True)

    L = gc.shape[0]
    t_idx = lax.broadcasted_iota(I32, (L, L), 0)
    s_idx = lax.broadcasted_iota(I32, (L, L), 1)
    causal = s_idx <= t_idx
    bcum_col = jnp.sum(jnp.where(causal, f_row, 0.0), axis=1, keepdims=True)
    bcum_row = jnp.sum(jnp.where(t_idx <= s_idx, f_col, 0.0), axis=0, keepdims=True)

    m_prev = m_scr[0:1, 0:1]
    a_log = bcum_col + m_prev
    d_log = jnp.where(causal, bcum_col - bcum_row + i_row, -jnp.inf)
    m_t = jnp.maximum(a_log, jnp.max(d_log, axis=1, keepdims=True))
    w_intra = jnp.exp(d_log - m_t)
    w_inter = jnp.exp(a_log - m_t)

    q = q_ref[...]
    k = k_ref[...]
    v = v_ref[...]
    q32 = q.astype(F32)
    k32 = k.astype(F32)
    c_mat = c_scr[...]
    n_row = n_scr[...]
    sc = lax.dot_general(q, k, (((1,), (1,)), ((), ())), preferred_element_type=F32) * (w_intra * scale)
    num = (w_inter * jnp.dot(q, c_mat.astype(BF16), preferred_element_type=F32)
           + jnp.dot(sc.astype(BF16), v, preferred_element_type=F32))
    den = (w_inter * jnp.sum(q32 * n_row, axis=1, keepdims=True) + jnp.sum(sc, axis=1, keepdims=True))
    h_out = num / jnp.maximum(jnp.abs(den), jnp.exp(-m_t))

    b_last = bcum_col[L - 1:L, :]
    g_log = b_last - bcum_col + i_col
    m_new = jnp.maximum(b_last + m_prev, jnp.max(g_log, axis=0, keepdims=True))
    decay = jnp.exp(b_last + m_prev - m_new)
    wk = jnp.exp(g_log - m_new) * scale
    wv = (wk * v.astype(F32)).astype(BF16)
    kv = lax.dot_general(k, wv, (((0,), (0,)), ((), ())), preferred_element_type=F32)
    c_scr[...] = decay * c_mat + kv
    n_scr[...] = decay * n_row + jnp.sum(wk * k32, axis=0, keepdims=True)
    m_scr[...] = jnp.broadcast_to(m_new, m_scr.shape)

    hn = h_out * lax.rsqrt(jnp.mean(h_out * h_out, axis=-1, keepdims=True) + EPS) * gh_ref[...]
    out_ref[...] = (_sigmoid(o_ref[...].astype(F32)) * hn).astype(out_ref.dtype)


def _mlstm(z, gcol, grow, g_head, bsz, seq, n_heads, dk, dv):
    t = bsz * seq
    L = ML_CHUNK
    nc = seq // L
    qk_w = n_heads * dk
    ng = 2 * n_heads
    k_off = qk_w // dk
    v_off = (2 * qk_w) // dv
    o_off = (2 * qk_w + n_heads * dv) // dv
    return pl.pallas_call(
        functools.partial(_mlstm_kernel, n_heads=n_heads, scale=dk ** -0.5),
        grid=(bsz, n_heads, nc),
        in_specs=[pl.BlockSpec((L, dk), lambda b, h, c: (b * nc + c, h)),
                  pl.BlockSpec((L, dk), lambda b, h, c: (b * nc + c, k_off + h)),
                  pl.BlockSpec((L, dv), lambda b, h, c: (b * nc + c, v_off + h)),
                  pl.BlockSpec((L, dv), lambda b, h, c: (b * nc + c, o_off + h)),
                  pl.BlockSpec((L, ng), lambda b, h, c: (b * nc + c, 0)),
                  pl.BlockSpec((ng, L), lambda b, h, c: (0, b * nc + c)),
                  pl.BlockSpec((1, dv), lambda b, h, c: (0, h))],
        out_specs=pl.BlockSpec((L, dv), lambda b, h, c: (b * nc + c, h)),
        out_shape=jax.ShapeDtypeStruct((t, n_heads * dv), BF16),
        scratch_shapes=[pltpu.VMEM((dk, dv), F32), pltpu.VMEM((1, dk), F32), pltpu.VMEM((1, LANES), F32)],
        compiler_params=_params(("parallel", "parallel", "arbitrary"), 32),
        name="ml_chunk",
    )(z, z, z, z, gcol, grow, g_head.reshape(1, n_heads * dv))


def _outproj_kernel(a_ref, w_ref, xres_ref, o_ref):
    o_ref[...] = xres_ref[...] + jnp.dot(a_ref[...], w_ref[...], preferred_element_type=F32)


def _outproj(a_bf, w_bf, x2, tm=1024, tn=512):
    t, kdim = a_bf.shape
    d = w_bf.shape[1]
    return pl.pallas_call(
        _outproj_kernel,
        grid=(t // tm, d // tn),
        in_specs=[pl.BlockSpec((tm, kdim), lambda i, j: (i, 0)),
                  pl.BlockSpec((kdim, tn), lambda i, j: (0, j)),
                  pl.BlockSpec((tm, tn), lambda i, j: (i, j))],
        out_specs=pl.BlockSpec((tm, tn), lambda i, j: (i, j)),
        out_shape=jax.ShapeDtypeStruct((t, d), F32),
        compiler_params=_params(("parallel", "arbitrary"), 40),
        name="ml_outproj",
    )(a_bf, w_bf, x2)


def _route_kernel(x_ref, nw_ref, wt_ref, b_ref, h_ref, eidx_ref, gate_ref, *, n_groups, per_group):
    h = _rms(x_ref[...], nw_ref[...])
    h_ref[...] = _pack_bf16(h)
    logits = lax.dot_general(wt_ref[...], h, (((1,), (1,)), ((), ())),
                             preferred_element_type=F32, precision=HIGHEST) + b_ref[...]
    n_exp = n_groups * per_group
    lg = logits[0:n_groups, :]
    le = logits[n_groups:n_groups + n_exp, :]
    g_iota = lax.broadcasted_iota(I32, lg.shape, 0)
    g_max = jnp.max(lg, axis=0, keepdims=True)
    g_sel = jnp.min(jnp.where(lg == g_max, g_iota, n_groups), axis=0, keepdims=True)
    p_sel = 1.0 / jnp.sum(jnp.exp(lg - g_max), axis=0, keepdims=True)

    e_iota = lax.broadcasted_iota(I32, le.shape, 0)
    lo = g_sel * per_group
    cand = jnp.where((e_iota >= lo) & (e_iota < lo + per_group), le, -jnp.inf)
    v1 = jnp.max(cand, axis=0, keepdims=True)
    i1 = jnp.min(jnp.where(cand == v1, e_iota, n_exp), axis=0, keepdims=True)
    cand2 = jnp.where(e_iota == i1, -jnp.inf, cand)
    v2 = jnp.max(cand2, axis=0, keepdims=True)
    i2 = jnp.min(jnp.where(cand2 == v2, e_iota, n_exp), axis=0, keepdims=True)
    e2 = jnp.exp(v2 - v1)
    inv = p_sel / (1.0 + e2)
    eidx_ref[...] = jnp.concatenate([i1, i2], axis=0)
    gate_ref[...] = jnp.concatenate([inv, e2 * inv], axis=0)


def _route(x2, nw, w_group, b_group, w_expert, b_expert, tm=512):
    t, d = x2.shape
    n_groups = w_group.shape[1]
    n_exp = w_expert.shape[1]
    n_rows = -(-(n_groups + n_exp) // LANES) * LANES
    wt = jnp.zeros((n_rows, d), F32).at[:n_groups].set(w_group.T.astype(F32))
    wt = wt.at[n_groups:n_groups + n_exp].set(w_expert.T.astype(F32))
    bias = jnp.zeros((n_rows, 1), F32).at[:n_groups, 0].set(b_group.astype(F32))
    bias = bias.at[n_groups:n_groups + n_exp, 0].set(b_expert.astype(F32))
    return pl.pallas_call(
        functools.partial(_route_kernel, n_groups=n_groups, per_group=n_exp // n_groups),
        grid=(t // tm,),
        in_specs=[pl.BlockSpec((tm, d), lambda i: (i, 0)),
                  pl.BlockSpec((1, d), lambda i: (0, 0)),
                  pl.BlockSpec((n_rows, d), lambda i: (0, 0)),
                  pl.BlockSpec((n_rows, 1), lambda i: (0, 0))],
        out_specs=[pl.BlockSpec((tm, d // 2), lambda i: (i, 0)),
                   pl.BlockSpec((MOE_TOPK, tm), lambda i: (0, i)),
                   pl.BlockSpec((MOE_TOPK, tm), lambda i: (0, i))],
        out_shape=[jax.ShapeDtypeStruct((t, d // 2), I32),
                   jax.ShapeDtypeStruct((MOE_TOPK, t), I32),
                   jax.ShapeDtypeStruct((MOE_TOPK, t), F32)],
        compiler_params=_params(("parallel",), 40),
        name="moe_route",
    )(x2, nw.reshape(1, d), wt, bias)


def _meta_kernel(eidx_ref, dest_ref, first_ref, nblk_ref, *, n_exp, n_tiles):
    tl = eidx_ref.shape[1]
    e_iota = lax.broadcasted_iota(I32, (n_exp, tl), 0)
    upper = (lax.broadcasted_iota(I32, (tl, tl), 0) < lax.broadcasted_iota(I32, (tl, tl), 1)).astype(BF16)

    def hits(i):
        return [e_iota == eidx_ref[pl.ds(k * n_tiles + i, 1), :] for k in range(MOE_TOPK)]

    def rank_tile(i, carry):
        hit = hits(i)
        member = sum(h.astype(F32) for h in hit)
        before = jnp.dot(member.astype(BF16), upper, preferred_element_type=F32) + carry
        for k in range(MOE_TOPK):
            rank = jnp.sum(jnp.where(hit[k], before, 0.0), axis=0, keepdims=True)
            dest_ref[pl.ds(k * n_tiles + i, 1), :] = rank.astype(I32)
        return carry + jnp.sum(member, axis=1, keepdims=True)

    counts = lax.fori_loop(0, n_tiles, rank_tile, jnp.zeros((n_exp, 1), F32)).astype(I32)
    nblk = (counts + (MOE_ROWS - 1)) >> MOE_ROWS_LOG2
    lower = (lax.broadcasted_iota(I32, (n_exp, n_exp), 1) < lax.broadcasted_iota(I32, (n_exp, n_exp), 0))
    first = jnp.dot(lower.astype(F32), jnp.broadcast_to(nblk.astype(F32), (n_exp, LANES)),
                    preferred_element_type=F32, precision=HIGHEST)[:, 0:1].astype(I32)
    pad_start = first << MOE_ROWS_LOG2

    def dest_tile(i, carry):
        hit = hits(i)
        for k in range(MOE_TOPK):
            base = jnp.sum(jnp.where(hit[k], pad_start, 0), axis=0, keepdims=True)
            row = pl.ds(k * n_tiles + i, 1)
            dest_ref[row, :] = dest_ref[row, :] + base
        return carry

    lax.fori_loop(0, n_tiles, dest_tile, 0)
    first_ref[...] = first
    nblk_ref[...] = nblk


def _meta(eidx, n_exp, tl=512):
    n_tok = eidx.shape[1]
    n_tiles = n_tok // tl
    dest, first, nblk = pl.pallas_call(
        functools.partial(_meta_kernel, n_exp=n_exp, n_tiles=n_tiles),
        out_shape=[jax.ShapeDtypeStruct((MOE_TOPK * n_tiles, tl), I32),
                   jax.ShapeDtypeStruct((n_exp, 1), I32),
                   jax.ShapeDtypeStruct((n_exp, 1), I32)],
        compiler_params=pltpu.CompilerParams(vmem_limit_bytes=32 * 1024 * 1024),
        name="moe_meta",
    )(eidx.reshape(MOE_TOPK * n_tiles, tl))
    return dest.reshape(MOE_TOPK, n_tok), first[:, 0], nblk[:, 0]


def _row_copy(src_ref, src_row, dst_ref, dst_row, sem):
    return pltpu.make_async_copy(src_ref.at[pl.ds(src_row, 1), :], dst_ref.at[pl.ds(dst_row, 1), :], sem)


def _dispatch_kernel(d0_ref, d1_ref, first_ref, nblk_ref, h_ref, xs_ref, zero_scr, sem, *, tt, n_exp):
    base = pl.program_id(0) * tt
    dests = (d0_ref, d1_ref)

    def pad_copy(e):
        last = pl.multiple_of((first_ref[e] + nblk_ref[e] - 1) * MOE_ROWS, MOE_ROWS)
        return pltpu.make_async_copy(zero_scr, xs_ref.at[pl.ds(last, MOE_ROWS), :], sem)

    def tail_copy(b):
        return pltpu.make_async_copy(
            zero_scr, xs_ref.at[pl.ds(pl.multiple_of(b * MOE_ROWS, MOE_ROWS), MOE_ROWS), :], sem)

    @pl.when(pl.program_id(0) == 0)
    def _():
        zero_scr[...] = jnp.zeros_like(zero_scr)
        for e in range(n_exp):
            @pl.when(nblk_ref[e] > 0)
            def _():
                pad_copy(e).start()
        for e in range(n_exp):
            @pl.when(nblk_ref[e] > 0)
            def _():
                pad_copy(e).wait()
        n_used = first_ref[n_exp - 1] + nblk_ref[n_exp - 1]
        n_blocks = xs_ref.shape[0] // MOE_ROWS
        lax.fori_loop(n_used, n_blocks, lambda b, c: (tail_copy(b).start(), c)[1], 0)
        lax.fori_loop(n_used, n_blocks, lambda b, c: (tail_copy(b).wait(), c)[1], 0)

    def start(t, c):
        for dref in dests:
            _row_copy(h_ref, t, xs_ref, dref[base + t], sem).start()
        return c

    def wait(t, c):
        for dref in dests:
            _row_copy(h_ref, t, xs_ref, dref[base + t], sem).wait()
        return c

    lax.fori_loop(0, tt, start, 0, unroll=DMA_UNROLL)
    lax.fori_loop(0, tt, wait, 0, unroll=DMA_UNROLL)


def _dispatch(h, dest, first, nblk, n_slots, tt=1024):
    n_tok, d = h.shape
    return pl.pallas_call(
        functools.partial(_dispatch_kernel, tt=tt, n_exp=first.shape[0]),
        grid_spec=pltpu.PrefetchScalarGridSpec(
            num_scalar_prefetch=4,
            grid=(n_tok // tt,),
            in_specs=[pl.BlockSpec((tt, d), lambda i, *_: (i, 0))],
            out_specs=pl.BlockSpec(memory_space=pl.ANY),
            scratch_shapes=[pltpu.VMEM((MOE_ROWS, d), h.dtype), pltpu.SemaphoreType.DMA(())]),
        out_shape=jax.ShapeDtypeStruct((n_slots, d), h.dtype),
        compiler_params=_params(("arbitrary",), 40),
        name="moe_dispatch",
    )(dest[0], dest[1], first, nblk, h)


def _expert_kernel(first_ref, nblk_ref, xs_ref, wgu_hbm, wdn_hbm, ys_ref, wgu_stage, wdn_stage, wgu_scr, wdn_scr,
                   xbuf, ybuf, sem_w, sem_x, sem_y, *, layer, hidden):
    e = pl.program_id(0)
    n_exp = pl.num_programs(0)
    nb = nblk_ref[e]
    first = first_ref[e]

    def w_copies(ex):
        copies = []
        for hbm, stage in ((wgu_hbm, wgu_stage), (wdn_hbm, wdn_stage)):
            rows = stage.shape[0] // W_DMA_CHUNKS
            for c in range(W_DMA_CHUNKS):
                sl = pl.ds(c * rows, rows)
                copies.append(pltpu.make_async_copy(hbm.at[layer, ex, sl, :], stage.at[sl, :],
                                                    sem_w.at[len(copies)]))
        return copies

    def x_copy(b, slot):
        rows = pl.ds(pl.multiple_of((first + b) * MOE_ROWS, MOE_ROWS), MOE_ROWS)
        return pltpu.make_async_copy(xs_ref.at[rows, :], xbuf.at[slot], sem_x.at[slot])

    def y_copy(b, slot):
        rows = pl.ds(pl.multiple_of((first + b) * MOE_ROWS, MOE_ROWS), MOE_ROWS)
        return pltpu.make_async_copy(ybuf.at[slot], ys_ref.at[rows, :], sem_y.at[slot])

    @pl.when(e == 0)
    def _():
        for c in w_copies(0):
            c.start()

    @pl.when(nb > 0)
    def _():
        x_copy(0, 0).start()

    for c in w_copies(e):
        c.wait()
    wgu_scr[...] = wgu_stage[...].astype(BF16)
    wdn_scr[...] = wdn_stage[...].astype(BF16)

    @pl.when(e + 1 < n_exp)
    def _():
        for c in w_copies(e + 1):
            c.start()

    def block(b, carry):
        slot = b & 1
        x_copy(b, slot).wait()

        @pl.when(b + 1 < nb)
        def _():
            x_copy(b + 1, 1 - slot).start()

        @pl.when(b >= 2)
        def _():
            y_copy(b - 2, slot).wait()

        gu = jnp.dot(_unpack_bf16(xbuf[slot]).astype(BF16), wgu_scr[...], preferred_element_type=F32)
        g = gu[:, :hidden]
        act = (g * _sigmoid(g)) * gu[:, hidden:]
        ybuf[slot] = _pack_bf16(jnp.dot(act.astype(BF16), wdn_scr[...], preferred_element_type=F32))
        y_copy(b, slot).start()
        return carry

    lax.fori_loop(0, nb, block, 0)

    @pl.when(nb >= 2)
    def _():
        y_copy(nb - 2, nb & 1).wait()

    @pl.when(nb >= 1)
    def _():
        y_copy(nb - 1, (nb - 1) & 1).wait()

    @pl.when(e == n_exp - 1)
    def _():
        ybuf[0] = jnp.zeros(ybuf.shape[1:], ybuf.dtype)
        n_blocks = ys_ref.shape[0] // MOE_ROWS
        lax.fori_loop(nb, n_blocks - first, lambda b, c: (y_copy(b, 0).start(), c)[1], 0)
        lax.fori_loop(nb, n_blocks - first, lambda b, c: (y_copy(b, 0).wait(), c)[1], 0)


def _experts(xs, first, nblk, layer, w_gu, w_down):
    n_exp, d, two_f = w_gu.shape[1], w_gu.shape[2], w_gu.shape[3]
    hidden = two_f // 2
    return pl.pallas_call(
        functools.partial(_expert_kernel, layer=layer, hidden=hidden),
        grid_spec=pltpu.PrefetchScalarGridSpec(
            num_scalar_prefetch=2,
            grid=(n_exp,),
            in_specs=[pl.BlockSpec(memory_space=pl.ANY),
                      pl.BlockSpec(memory_space=pl.ANY),
                      pl.BlockSpec(memory_space=pl.ANY)],
            out_specs=pl.BlockSpec(memory_space=pl.ANY),
            scratch_shapes=[pltpu.VMEM((d, two_f), w_gu.dtype), pltpu.VMEM((hidden, d), w_down.dtype),
                            pltpu.VMEM((d, two_f), BF16), pltpu.VMEM((hidden, d), BF16),
                            pltpu.VMEM((2, MOE_ROWS) + xs.shape[1:], xs.dtype),
                            pltpu.VMEM((2, MOE_ROWS) + xs.shape[1:], xs.dtype),
                            pltpu.SemaphoreType.DMA((2 * W_DMA_CHUNKS,)), pltpu.SemaphoreType.DMA((2,)),
                            pltpu.SemaphoreType.DMA((2,))]),
        out_shape=jax.ShapeDtypeStruct(xs.shape, xs.dtype),
        compiler_params=_params(("arbitrary",), 52),
        name="moe_experts",
    )(first, nblk, xs, w_gu, w_down)


def _combine_kernel(d0_ref, d1_ref, x_ref, g_ref, nw_ref, ys_ref, o_ref, buf, sem, *, tt, final_norm):
    base = pl.program_id(0) * tt
    dests = (d0_ref, d1_ref)

    def start(t, c):
        for k, dref in enumerate(dests):
            _row_copy(ys_ref, dref[base + t], buf.at[k], t, sem).start()
        return c

    def wait(t, c):
        for k, dref in enumerate(dests):
            _row_copy(ys_ref, dref[base + t], buf.at[k], t, sem).wait()
        return c

    lax.fori_loop(0, tt, start, 0, unroll=DMA_UNROLL)
    lax.fori_loop(0, tt, wait, 0, unroll=DMA_UNROLL)
    g = g_ref[...]
    out = x_ref[...] + sum(g[:, k:k + 1] * _unpack_bf16(buf[k]) for k in range(MOE_TOPK))
    o_ref[...] = _rms(out, nw_ref[...]) if final_norm else out


def _combine(x2, ys, dest, gates_t, final_w, tt=256):
    t, d = x2.shape
    nw = jnp.ones((1, d), F32) if final_w is None else final_w.reshape(1, d).astype(F32)
    return pl.pallas_call(
        functools.partial(_combine_kernel, tt=tt, final_norm=final_w is not None),
        grid_spec=pltpu.PrefetchScalarGridSpec(
            num_scalar_prefetch=2,
            grid=(t // tt,),
            in_specs=[pl.BlockSpec((tt, d), lambda i, d0, d1: (i, 0)),
                      pl.BlockSpec((tt, MOE_TOPK), lambda i, d0, d1: (i, 0)),
                      pl.BlockSpec((1, d), lambda i, d0, d1: (0, 0)),
                      pl.BlockSpec(memory_space=pl.ANY)],
            out_specs=pl.BlockSpec((tt, d), lambda i, d0, d1: (i, 0)),
            scratch_shapes=[pltpu.VMEM((MOE_TOPK, tt) + ys.shape[1:], ys.dtype), pltpu.SemaphoreType.DMA(())]),
        out_shape=jax.ShapeDtypeStruct((t, d), F32),
        compiler_params=_params(("arbitrary",), 40),
        name="moe_combine",
    )(dest[0], dest[1], x2, gates_t, nw, ys)


def _hier_moe(x2, layer, nw, w_group, b_group, w_expert, b_expert, w_gu, w_down, final_w):
    t = x2.shape[0]
    n_exp = w_expert.shape[1]
    n_blocks = (t * MOE_TOPK) // MOE_ROWS + n_exp
    h, eidx, gates = _route(x2, nw, w_group, b_group, w_expert, b_expert)
    dest, first, nblk = _meta(eidx, n_exp)
    xs = _dispatch(h, dest, first, nblk, n_blocks * MOE_ROWS)
    ys = _experts(xs, first, nblk, layer, w_gu, w_down)
    return _combine(x2, ys, dest, gates.T, final_w)


def kernel(x, s5_lam_re, s5_lam_im, s5_log_dt, s5_b_re, s5_b_im, s5_c_re, s5_c_im, s5_d, s5_w_glu, ml_w_in, ml_b_gate, ml_g_head, ml_w_out, norm_mix, norm_ffn, moe_w_group, moe_b_group, moe_w_expert, moe_b_expert, moe_w_gu, moe_w_down, norm_final):
    bsz, seq, d = x.shape
    assert norm_mix.shape[0] == 2 and s5_w_glu.shape[0] == 1 and ml_w_in.shape[0] == 1
    x2 = x.reshape(bsz * seq, d).astype(F32)

    ops = _s5_operators(s5_lam_re[0], s5_lam_im[0], s5_log_dt[0], s5_b_re[0], s5_b_im[0],
                        s5_c_re[0], s5_c_im[0], seq // S5_CHUNK)
    g = _s5_mix(x2, _row_scale(x2), norm_mix[0].astype(F32), s5_d[0].astype(F32), ops, bsz, seq)
    x2 = _glu(x2, g, s5_w_glu[0].astype(BF16))
    x2 = _hier_moe(x2, 0, norm_ffn[0], moe_w_group[0], moe_b_group[0], moe_w_expert[0], moe_b_expert[0],
                   moe_w_gu, moe_w_down, None)

    n_heads = ml_b_gate.shape[1] // 2
    dv = ml_w_out.shape[1] // n_heads
    dk = (ml_w_in.shape[2] - 2 * n_heads * dv - 2 * n_heads) // (2 * n_heads)
    n_main = 2 * n_heads * (dk + dv)
    z = _inproj(x2, norm_mix[1], ml_w_in[0, :, :n_main].astype(BF16))
    gcol = _gates(x2, norm_mix[1], ml_w_in[0, :, n_main:].astype(F32), ml_b_gate[0].astype(F32))
    a_bf = _mlstm(z, gcol, gcol.T, ml_g_head[0].astype(F32), bsz, seq, n_heads, dk, dv)
    x2 = _outproj(a_bf, ml_w_out[0].astype(BF16), x2)
    x2 = _hier_moe(x2, 1, norm_ffn[1], moe_w_group[1], moe_b_group[1], moe_w_expert[1], moe_b_expert[1],
                   moe_w_gu, moe_w_down, norm_final)
    return x2.reshape(bsz, seq, d).astype(x.dtype)
```

```python
import functools
import math

import jax
import jax.numpy as jnp
from jax import lax
from jax.experimental import pallas as pl
from jax.experimental.pallas import tpu as pltpu

F32 = jnp.float32
BF16 = jnp.bfloat16
I32 = jnp.int32

EPS = 1e-6
GATE_SOFTCAP = 15.0
MOE_TOPK = 2

LANES = 128
S5_CHUNK = 16
S5_OUT_STEPS = 2
S5_BATCH = 2
ML_CHUNK = 256
ML_HEAD_GROUP = 4
MOE_ROWS_LOG2 = 8
MOE_ROWS = 1 << MOE_ROWS_LOG2
VMEM_BYTES = 64 * 1024 * 1024
DMA_UNROLL = 8
W_DMA_CHUNKS = 4
HIGHEST = lax.Precision.HIGHEST


def _params(sem, vmem_mib):
    assert vmem_mib * 1024 * 1024 < VMEM_BYTES
    return pltpu.CompilerParams(dimension_semantics=sem, vmem_limit_bytes=vmem_mib * 1024 * 1024)


def _rms(x32, w):
    ms = jnp.mean(x32 * x32, axis=-1, keepdims=True)
    return x32 * lax.rsqrt(ms + EPS) * w


def _sigmoid(x):
    return 1.0 / (1.0 + jnp.exp(-x))


def _gelu_tanh(x):
    c = math.sqrt(2.0 / math.pi)
    return 0.5 * x * (1.0 + jnp.tanh(c * (x + 0.044715 * (x * x * x))))


HI16 = -65536
LO16 = 65535


def _pack_bf16(v):
    half = v.shape[1] // 2
    bits = lax.bitcast_convert_type(v.astype(BF16).astype(F32), I32)
    return (bits[:, half:] & HI16) | ((bits[:, :half] >> 16) & LO16)


def _unpack_bf16(w):
    lo = lax.bitcast_convert_type(w << 16, F32)
    hi = lax.bitcast_convert_type(w & HI16, F32)
    return jnp.concatenate([lo, hi], axis=1)


def _row_scale_kernel(x_ref, r_ref):
    x = x_ref[...]
    r_ref[...] = lax.rsqrt(jnp.mean(x * x, axis=-1, keepdims=True) + EPS)


def _row_scale(x2, tm=512):
    t, d = x2.shape
    return pl.pallas_call(
        _row_scale_kernel,
        grid=(t // tm,),
        in_specs=[pl.BlockSpec((tm, d), lambda i: (i, 0))],
        out_specs=pl.BlockSpec((tm, 1), lambda i: (i, 0)),
        out_shape=jax.ShapeDtypeStruct((t, 1), F32),
        compiler_params=_params(("parallel",), 32),
        name="row_scale",
    )(x2)


def _s5_build_operators(bexp_ref, cexp_ref, qpow_ref, ppow_ref, win_scr, wout_scr, toep_scr):
    L = qpow_ref.shape[1]
    half = bexp_ref.shape[2] // 2

    def tile(ref, s, t):
        return ref.at[s * LANES:(s + 1) * LANES, t * LANES:(t + 1) * LANES]

    br, bi = bexp_ref[0, :, :half], bexp_ref[0, :, half:]
    cr, ci = cexp_ref[0, :half, :], cexp_ref[0, half:, :]
    b_cat = bexp_ref[0].astype(BF16)
    for s in range(L):
        qr, qi = qpow_ref[0, s:s + 1, :half], qpow_ref[0, s:s + 1, half:]
        win_scr[s * LANES:(s + 1) * LANES, :] = jnp.concatenate(
            [br * qr - bi * qi, br * qi + bi * qr], axis=1).astype(BF16)
    for t in range(L + 1):
        pr, pi = ppow_ref[0, :half, t:t + 1], ppow_ref[0, half:, t:t + 1]
        w_t = jnp.concatenate([cr * pr - ci * pi, -(cr * pi + ci * pr)], axis=0).astype(BF16)
        if t >= 1:
            wout_scr[:, (t - 1) * LANES:t * LANES] = w_t
        if t < L:
            tap = jnp.dot(b_cat, w_t, preferred_element_type=F32).astype(BF16)
            for s in range(L - t):
                tile(toep_scr, s, s + t)[...] = tap
    zero = jnp.zeros((LANES, LANES), BF16)
    for s in range(L):
        for tau in range(s):
            tile(toep_scr, s, tau)[...] = zero


def _s5_kernel(x_ref, r_ref, nw_ref, dsk_ref, bexp_ref, cexp_ref, qpow_ref, ppow_ref, lpow_ref, g_ref,
               win_scr, wout_scr, toep_scr, u_scr, *, n_chunks):
    L = qpow_ref.shape[1]
    rows = u_scr.shape[0]

    @pl.when(pl.program_id(1) == 0)
    def _():
        _s5_build_operators(bexp_ref, cexp_ref, qpow_ref, ppow_ref, win_scr, wout_scr, toep_scr)

    def h_step(s):
        return x_ref[pl.ds(s, rows, stride=L), :] * r_ref[:, s:s + 1] * nw_ref[...]

    for s in range(L):
        u_scr[:, s * LANES:(s + 1) * LANES] = h_step(s).astype(BF16)
    v = jnp.dot(u_scr[...], win_scr[...], preferred_element_type=F32)
    half = v.shape[1] // 2
    assert n_chunks & (n_chunks - 1) == 0 and rows % n_chunks == 0
    cidx = lax.broadcasted_iota(I32, (rows, 1), 0) & (n_chunks - 1)

    def shifted(a, d):
        return jnp.where(cidx >= d, pltpu.roll(a, d, 0), 0.0)

    xr = shifted(v[:, :half], 1)
    xi = shifted(v[:, half:], 1)
    d, k = 1, 0
    while d < n_chunks:
        ar = lpow_ref[0, k:k + 1, :half]
        ai = lpow_ref[0, k:k + 1, half:]
        sr = shifted(xr, d)
        si = shifted(xi, d)
        xr, xi = xr + (ar * sr - ai * si), xi + (ar * si + ai * sr)
        d, k = 2 * d, k + 1
    state = jnp.concatenate([xr, xi], axis=1).astype(BF16)
    for j in range(L // S5_OUT_STEPS):
        lo, hi = j * S5_OUT_STEPS * LANES, (j + 1) * S5_OUT_STEPS * LANES
        y = (jnp.dot(u_scr[:, :hi], toep_scr[:hi, lo:hi], preferred_element_type=F32)
             + jnp.dot(state, wout_scr[:, lo:hi], preferred_element_type=F32))
        for k in range(S5_OUT_STEPS):
            tau = j * S5_OUT_STEPS + k
            g_ref[pl.ds(tau, rows, stride=L), :] = _gelu_tanh(
                y[:, k * LANES:(k + 1) * LANES] + dsk_ref[...] * h_step(tau))


def _s5_operators(lam_re, lam_im, log_dt, b_re, b_im, c_re, c_im, n_chunks):
    g, p, cg = b_re.shape
    L = S5_CHUNK
    gt = LANES // cg
    nt = g // gt
    lam = lax.complex(lam_re.astype(F32), lam_im.astype(F32))
    dt = jnp.exp(log_dt.astype(F32))[:, None]
    z = lam * dt
    lam_bar = jnp.exp(z)
    b_bar = ((lam_bar - 1.0) / lam)[..., None] * lax.complex(b_re.astype(F32), b_im.astype(F32))
    cc = lax.complex(c_re.astype(F32), c_im.astype(F32))
    steps = jnp.arange(L + 1, dtype=F32)
    pw = jnp.exp(z[:, None, :] * steps[None, :, None])
    eye = jnp.eye(gt, dtype=F32)

    def re_im(a, axis):
        return jnp.stack([jnp.real(a), jnp.imag(a)], axis=axis)

    def per_tile(a):
        k = a.shape[1]
        return jnp.transpose(a.reshape(nt, gt, k, p), (0, 2, 1, 3)).reshape(nt, k, gt * p)

    bexp = re_im(jnp.transpose(b_bar, (0, 2, 1)), 2).reshape(nt, gt, cg, 2, p)
    bexp = jnp.einsum('iacrp,ab->iacrbp', bexp, eye).reshape(nt, gt * cg, 2 * gt * p)
    cexp = re_im(jnp.transpose(cc, (0, 2, 1)), 1).reshape(nt, gt, 2, p, cg)
    cexp = jnp.einsum('iarpc,ab->irapbc', cexp, eye).reshape(nt, 2 * gt * p, gt * cg)
    q = pw[:, L - 1 - jnp.arange(L), :]
    qpow = jnp.concatenate([per_tile(jnp.real(q)), per_tile(jnp.imag(q))], axis=-1)
    pt = jnp.concatenate([per_tile(jnp.real(pw)), per_tile(jnp.imag(pw))], axis=-1)
    ppow = jnp.zeros((nt, 2 * gt * p, LANES), F32).at[:, :, :L + 1].set(jnp.transpose(pt, (0, 2, 1)))

    n_steps = max(1, (n_chunks - 1).bit_length())
    hops = (2.0 ** jnp.arange(n_steps, dtype=F32)) * L
    lp = jnp.exp(z[:, None, :] * hops[None, :, None])
    lpow = jnp.concatenate([per_tile(jnp.real(lp)), per_tile(jnp.imag(lp))], axis=-1)
    return bexp, cexp, qpow, ppow, lpow


def _s5_mix(x2, r, nw, d_skip, ops, bsz, seq):
    t, d = x2.shape
    L = S5_CHUNK
    n_chunks = seq // L
    nt = d // LANES
    n_state = ops[0].shape[2]
    r2 = r.reshape(t // L, L)
    bb = S5_BATCH if bsz % S5_BATCH == 0 else 1
    rows = bb * n_chunks
    return pl.pallas_call(
        functools.partial(_s5_kernel, n_chunks=n_chunks),
        grid=(nt, bsz // bb),
        in_specs=[pl.BlockSpec((bb * seq, LANES), lambda i, b: (b, i)),
                  pl.BlockSpec((rows, L), lambda i, b: (b, 0)),
                  pl.BlockSpec((1, LANES), lambda i, b: (0, i)),
                  pl.BlockSpec((1, LANES), lambda i, b: (0, i))]
                 + [pl.BlockSpec((1,) + a.shape[1:], lambda i, b: (i, 0, 0)) for a in ops],
        out_specs=pl.BlockSpec((bb * seq, LANES), lambda i, b: (b, i)),
        out_shape=jax.ShapeDtypeStruct((t, d), F32),
        scratch_shapes=[pltpu.VMEM((L * LANES, n_state), BF16), pltpu.VMEM((n_state, L * LANES), BF16),
                        pltpu.VMEM((L * LANES, L * LANES), BF16), pltpu.VMEM((rows, L * LANES), BF16)],
        compiler_params=_params(("arbitrary", "arbitrary"), 56),
        name="s5_scan",
    )(x2, r2, nw.reshape(1, d), d_skip.reshape(1, d), *ops)


def _glu_kernel(g_ref, wv_ref, wg_ref, xres_ref, o_ref, g_scr):
    @pl.when(pl.program_id(1) == 0)
    def _():
        g_scr[...] = g_ref[...].astype(BF16)

    g = g_scr[...]
    val = jnp.dot(g, wv_ref[...], preferred_element_type=F32)
    gate = jnp.dot(g, wg_ref[...], preferred_element_type=F32)
    o_ref[...] = xres_ref[...] + val * _sigmoid(gate)


def _glu(x2, g, w_glu_bf, tm=1024, tn=512):
    t, d = x2.shape
    nj = d // tn
    return pl.pallas_call(
        _glu_kernel,
        grid=(t // tm, nj),
        in_specs=[pl.BlockSpec((tm, d), lambda i, j: (i, 0)),
                  pl.BlockSpec((d, tn), lambda i, j: (0, j)),
                  pl.BlockSpec((d, tn), lambda i, j: (0, j + nj)),
                  pl.BlockSpec((tm, tn), lambda i, j: (i, j))],
        out_specs=pl.BlockSpec((tm, tn), lambda i, j: (i, j)),
        out_shape=jax.ShapeDtypeStruct((t, d), F32),
        scratch_shapes=[pltpu.VMEM((tm, d), BF16)],
        compiler_params=_params(("parallel", "arbitrary"), 48),
        name="s5_glu",
    )(g, w_glu_bf, w_glu_bf, x2)


def _inproj_kernel(x_ref, nw_ref, w_ref, z_ref, h_scr):
    @pl.when(pl.program_id(1) == 0)
    def _():
        h_scr[...] = _rms(x_ref[...], nw_ref[...]).astype(BF16)

    z_ref[...] = jnp.dot(h_scr[...], w_ref[...], preferred_element_type=F32).astype(z_ref.dtype)


def _inproj(x2, nw, w_bf, tm=1024, tn=1024):
    t, d = x2.shape
    n = w_bf.shape[1]
    return pl.pallas_call(
        _inproj_kernel,
        grid=(t // tm, n // tn),
        in_specs=[pl.BlockSpec((tm, d), lambda i, j: (i, 0)),
                  pl.BlockSpec((1, d), lambda i, j: (0, 0)),
                  pl.BlockSpec((d, tn), lambda i, j: (0, j))],
        out_specs=pl.BlockSpec((tm, tn), lambda i, j: (i, j)),
        out_shape=jax.ShapeDtypeStruct((t, n), BF16),
        scratch_shapes=[pltpu.VMEM((tm, d), BF16)],
        compiler_params=_params(("parallel", "arbitrary"), 48),
        name="ml_inproj",
    )(x2, nw.reshape(1, d), w_bf)


def _gates_kernel(x_ref, nw_ref, w_ref, b_ref, g_ref, *, n_heads):
    h = _rms(x_ref[...], nw_ref[...])
    pre = jnp.dot(h, w_ref[...], preferred_element_type=F32, precision=HIGHEST) + b_ref[...]
    t = GATE_SOFTCAP * jnp.tanh(pre / GATE_SOFTCAP)
    log_sig = jnp.minimum(t, 0.0) - jnp.log(1.0 + jnp.exp(-jnp.abs(t)))
    g_ref[...] = jnp.where(lax.broadcasted_iota(I32, pre.shape, 1) >= n_heads, log_sig, t)


def _gates(x2, nw, w_gate, b_gate, tm=512):
    t, d = x2.shape
    ng = w_gate.shape[1]
    return pl.pallas_call(
        functools.partial(_gates_kernel, n_heads=ng // 2),
        grid=(t // tm,),
        in_specs=[pl.BlockSpec((tm, d), lambda i: (i, 0)),
                  pl.BlockSpec((1, d), lambda i: (0, 0)),
                  pl.BlockSpec((d, ng), lambda i: (0, 0)),
                  pl.BlockSpec((1, ng), lambda i: (0, 0))],
        out_specs=pl.BlockSpec((tm, ng), lambda i: (i, 0)),
        out_shape=jax.ShapeDtypeStruct((t, ng), F32),
        compiler_params=_params(("parallel",), 32),
        name="ml_gates",
    )(x2, nw.reshape(1, d), w_gate, b_gate.reshape(1, ng))


def _mlstm_head(head, q, k, v, o, gc, gr, g_head, c_ref, n_ref, m_ref, n_heads, scale):
    lane_h = lax.broadcasted_iota(I32, gc.shape, 1)
    sub_h = lax.broadcasted_iota(I32, gr.shape, 0)
    i_col = jnp.sum(jnp.where(lane_h == head, gc, 0.0), axis=1, keepdims=True)
    f_col = jnp.sum(jnp.where(lane_h == head + n_heads, gc, 0.0), axis=1, keepdims=True)
    i_row = jnp.sum(jnp.where(sub_h == head, gr, 0.0), axis=0, keepdims=True)
    f_row = jnp.sum(jnp.where(sub_h == head + n_heads, gr, 0.0), axis=0, keepdims=True)

    L = gc.shape[0]
    t_idx = lax.broadcasted_iota(I32, (L, L), 0)
    s_idx = lax.broadcasted_iota(I32, (L, L), 1)
    causal = s_idx <= t_idx
    bcum_col = jnp.sum(jnp.where(causal, f_row, 0.0), axis=1, keepdims=True)
    bcum_row = jnp.sum(jnp.where(t_idx <= s_idx, f_col, 0.0), axis=0, keepdims=True)

    m_prev = m_ref[0:1, 0:1]
    a_log = bcum_col + m_prev
    d_log = jnp.where(causal, bcum_col - bcum_row + i_row, -jnp.inf)
    m_t = jnp.maximum(a_log, jnp.max(d_log, axis=1, keepdims=True))
    w_intra = jnp.exp(d_log - m_t)
    w_inter = jnp.exp(a_log - m_t)

    q32 = q.astype(F32)
    k32 = k.astype(F32)
    c_mat = c_ref[...]
    n_row = n_ref[...]
    sc = lax.dot_general(q, k, (((1,), (1,)), ((), ())), preferred_element_type=F32) * (w_intra * scale)
    num = (w_inter * jnp.dot(q, c_mat.astype(BF16), preferred_element_type=F32)
           + jnp.dot(sc.astype(BF16), v, preferred_element_type=F32))
    den = (w_inter * jnp.sum(q32 * n_row, axis=1, keepdims=True) + jnp.sum(sc, axis=1, keepdims=True))
    h_out = num / jnp.maximum(jnp.abs(den), jnp.exp(-m_t))

    b_last = bcum_col[L - 1:L, :]
    g_log = b_last - bcum_col + i_col
    m_new = jnp.maximum(b_last + m_prev, jnp.max(g_log, axis=0, keepdims=True))
    decay = jnp.exp(b_last + m_prev - m_new)
    wk = jnp.exp(g_log - m_new) * scale
    wv = (wk * v.astype(F32)).astype(BF16)
    kv = lax.dot_general(k, wv, (((0,), (0,)), ((), ())), preferred_element_type=F32)
    c_ref[...] = decay * c_mat + kv
    n_ref[...] = decay * n_row + jnp.sum(wk * k32, axis=0, keepdims=True)
    m_ref[...] = jnp.broadcast_to(m_new, m_ref.shape)

    hn = h_out * lax.rsqrt(jnp.mean(h_out * h_out, axis=-1, keepdims=True) + EPS) * g_head
    return _sigmoid(o.astype(F32)) * hn


def _mlstm_kernel(q_ref, k_ref, v_ref, o_ref, gc_ref, gr_ref, gh_ref, out_ref, c_scr, n_scr, m_scr,
                  *, n_heads, scale):
    @pl.when(pl.program_id(2) == 0)
    def _():
        c_scr[...] = jnp.zeros_like(c_scr)
        n_scr[...] = jnp.zeros_like(n_scr)
        m_scr[...] = jnp.zeros_like(m_scr)

    group, dk, dv = c_scr.shape
    gc = gc_ref[...]
    gr = gr_ref[...]
    for j in range(group):
        qk_cols = slice(j * dk, (j + 1) * dk)
        v_cols = slice(j * dv, (j + 1) * dv)
        out = _mlstm_head(pl.program_id(1) * group + j, q_ref[:, qk_cols], k_ref[:, qk_cols], v_ref[:, v_cols],
                          o_ref[:, v_cols], gc, gr, gh_ref[:, v_cols], c_scr.at[j], n_scr.at[j], m_scr.at[j],
                          n_heads, scale)
        out_ref[:, v_cols] = out.astype(out_ref.dtype)


def _mlstm(z, gcol, grow, g_head, bsz, seq, n_heads, dk, dv):
    t = bsz * seq
    L = ML_CHUNK
    nc = seq // L
    hp = ML_HEAD_GROUP if n_heads % ML_HEAD_GROUP == 0 else 1
    qk_w = n_heads * dk
    ng = 2 * n_heads
    k_off = qk_w // (hp * dk)
    v_off = (2 * qk_w) // (hp * dv)
    o_off = (2 * qk_w + n_heads * dv) // (hp * dv)
    return pl.pallas_call(
        functools.partial(_mlstm_kernel, n_heads=n_heads, scale=dk ** -0.5),
        grid=(bsz, n_heads // hp, nc),
        in_specs=[pl.BlockSpec((L, hp * dk), lambda b, h, c: (b * nc + c, h)),
                  pl.BlockSpec((L, hp * dk), lambda b, h, c: (b * nc + c, k_off + h)),
                  pl.BlockSpec((L, hp * dv), lambda b, h, c: (b * nc + c, v_off + h)),
                  pl.BlockSpec((L, hp * dv), lambda b, h, c: (b * nc + c, o_off + h)),
                  pl.BlockSpec((L, ng), lambda b, h, c: (b * nc + c, 0)),
                  pl.BlockSpec((ng, L), lambda b, h, c: (0, b * nc + c)),
                  pl.BlockSpec((1, hp * dv), lambda b, h, c: (0, h))],
        out_specs=pl.BlockSpec((L, hp * dv), lambda b, h, c: (b * nc + c, h)),
        out_shape=jax.ShapeDtypeStruct((t, n_heads * dv), BF16),
        scratch_shapes=[pltpu.VMEM((hp, dk, dv), F32), pltpu.VMEM((hp, 1, dk), F32),
                        pltpu.VMEM((hp, 1, LANES), F32)],
        compiler_params=_params(("parallel", "parallel", "arbitrary"), 40),
        name="ml_chunk",
    )(z, z, z, z, gcol, grow, g_head.reshape(1, n_heads * dv))


def _outproj_kernel(a_ref, w_ref, xres_ref, o_ref):
    o_ref[...] = xres_ref[...] + jnp.dot(a_ref[...], w_ref[...], preferred_element_type=F32)


def _outproj(a_bf, w_bf, x2, tm=1024, tn=1024):
    t, kdim = a_bf.shape
    d = w_bf.shape[1]
    return pl.pallas_call(
        _outproj_kernel,
        grid=(t // tm, d // tn),
        in_specs=[pl.BlockSpec((tm, kdim), lambda i, j: (i, 0)),
                  pl.BlockSpec((kdim, tn), lambda i, j: (0, j)),
                  pl.BlockSpec((tm, tn), lambda i, j: (i, j))],
        out_specs=pl.BlockSpec((tm, tn), lambda i, j: (i, j)),
        out_shape=jax.ShapeDtypeStruct((t, d), F32),
        compiler_params=_params(("parallel", "arbitrary"), 40),
        name="ml_outproj",
    )(a_bf, w_bf, x2)


def _route_kernel(x_ref, nw_ref, wt_ref, b_ref, h_ref, eidx_ref, gate_ref, *, n_groups, per_group):
    h = _rms(x_ref[...], nw_ref[...])
    h_ref[...] = _pack_bf16(h)
    logits = lax.dot_general(wt_ref[...], h, (((1,), (1,)), ((), ())),
                             preferred_element_type=F32, precision=HIGHEST) + b_ref[...]
    n_exp = n_groups * per_group
    lg = logits[0:n_groups, :]
    le = logits[n_groups:n_groups + n_exp, :]
    g_iota = lax.broadcasted_iota(I32, lg.shape, 0)
    g_max = jnp.max(lg, axis=0, keepdims=True)
    g_sel = jnp.min(jnp.where(lg == g_max, g_iota, n_groups), axis=0, keepdims=True)
    p_sel = 1.0 / jnp.sum(jnp.exp(lg - g_max), axis=0, keepdims=True)

    e_iota = lax.broadcasted_iota(I32, le.shape, 0)
    lo = g_sel * per_group
    cand = jnp.where((e_iota >= lo) & (e_iota < lo + per_group), le, -jnp.inf)
    v1 = jnp.max(cand, axis=0, keepdims=True)
    i1 = jnp.min(jnp.where(cand == v1, e_iota, n_exp), axis=0, keepdims=True)
    cand2 = jnp.where(e_iota == i1, -jnp.inf, cand)
    v2 = jnp.max(cand2, axis=0, keepdims=True)
    i2 = jnp.min(jnp.where(cand2 == v2, e_iota, n_exp), axis=0, keepdims=True)
    e2 = jnp.exp(v2 - v1)
    inv = p_sel / (1.0 + e2)
    eidx_ref[...] = jnp.concatenate([i1, i2], axis=0)
    gate_ref[...] = jnp.concatenate([inv, e2 * inv], axis=0)


def _route(x2, nw, w_group, b_group, w_expert, b_expert, tm=512):
    t, d = x2.shape
    n_groups = w_group.shape[1]
    n_exp = w_expert.shape[1]
    n_rows = -(-(n_groups + n_exp) // LANES) * LANES
    wt = jnp.zeros((n_rows, d), F32).at[:n_groups].set(w_group.T.astype(F32))
    wt = wt.at[n_groups:n_groups + n_exp].set(w_expert.T.astype(F32))
    bias = jnp.zeros((n_rows, 1), F32).at[:n_groups, 0].set(b_group.astype(F32))
    bias = bias.at[n_groups:n_groups + n_exp, 0].set(b_expert.astype(F32))
    return pl.pallas_call(
        functools.partial(_route_kernel, n_groups=n_groups, per_group=n_exp // n_groups),
        grid=(t // tm,),
        in_specs=[pl.BlockSpec((tm, d), lambda i: (i, 0)),
                  pl.BlockSpec((1, d), lambda i: (0, 0)),
                  pl.BlockSpec((n_rows, d), lambda i: (0, 0)),
                  pl.BlockSpec((n_rows, 1), lambda i: (0, 0))],
        out_specs=[pl.BlockSpec((tm, d // 2), lambda i: (i, 0)),
                   pl.BlockSpec((MOE_TOPK, tm), lambda i: (0, i)),
                   pl.BlockSpec((MOE_TOPK, tm), lambda i: (0, i))],
        out_shape=[jax.ShapeDtypeStruct((t, d // 2), I32),
                   jax.ShapeDtypeStruct((MOE_TOPK, t), I32),
                   jax.ShapeDtypeStruct((MOE_TOPK, t), F32)],
        compiler_params=_params(("parallel",), 40),
        name="moe_route",
    )(x2, nw.reshape(1, d), wt, bias)


def _meta_kernel(eidx_ref, dest_ref, first_ref, nblk_ref, *, n_exp, n_tiles):
    tl = eidx_ref.shape[1]
    e_iota = lax.broadcasted_iota(I32, (n_exp, tl), 0)
    upper = (lax.broadcasted_iota(I32, (tl, tl), 0) < lax.broadcasted_iota(I32, (tl, tl), 1)).astype(BF16)

    def hits(i):
        return [e_iota == eidx_ref[pl.ds(k * n_tiles + i, 1), :] for k in range(MOE_TOPK)]

    def rank_tile(i, carry):
        hit = hits(i)
        member = sum(h.astype(F32) for h in hit)
        before = jnp.dot(member.astype(BF16), upper, preferred_element_type=F32) + carry
        for k in range(MOE_TOPK):
            rank = jnp.sum(jnp.where(hit[k], before, 0.0), axis=0, keepdims=True)
            dest_ref[pl.ds(k * n_tiles + i, 1), :] = rank.astype(I32)
        return carry + jnp.sum(member, axis=1, keepdims=True)

    counts = lax.fori_loop(0, n_tiles, rank_tile, jnp.zeros((n_exp, 1), F32)).astype(I32)
    nblk = (counts + (MOE_ROWS - 1)) >> MOE_ROWS_LOG2
    lower = (lax.broadcasted_iota(I32, (n_exp, n_exp), 1) < lax.broadcasted_iota(I32, (n_exp, n_exp), 0))
    first = jnp.dot(lower.astype(F32), jnp.broadcast_to(nblk.astype(F32), (n_exp, LANES)),
                    preferred_element_type=F32, precision=HIGHEST)[:, 0:1].astype(I32)
    pad_start = first << MOE_ROWS_LOG2

    def dest_tile(i, carry):
        hit = hits(i)
        for k in range(MOE_TOPK):
            base = jnp.sum(jnp.where(hit[k], pad_start, 0), axis=0, keepdims=True)
            row = pl.ds(k * n_tiles + i, 1)
            dest_ref[row, :] = dest_ref[row, :] + base
        return carry

    lax.fori_loop(0, n_tiles, dest_tile, 0)
    first_ref[...] = first
    nblk_ref[...] = nblk


def _meta(eidx, n_exp, tl=512):
    n_tok = eidx.shape[1]
    n_tiles = n_tok // tl
    dest, first, nblk = pl.pallas_call(
        functools.partial(_meta_kernel, n_exp=n_exp, n_tiles=n_tiles),
        out_shape=[jax.ShapeDtypeStruct((MOE_TOPK * n_tiles, tl), I32),
                   jax.ShapeDtypeStruct((n_exp, 1), I32),
                   jax.ShapeDtypeStruct((n_exp, 1), I32)],
        compiler_params=pltpu.CompilerParams(vmem_limit_bytes=32 * 1024 * 1024),
        name="moe_meta",
    )(eidx.reshape(MOE_TOPK * n_tiles, tl))
    return dest.reshape(MOE_TOPK, n_tok), first[:, 0], nblk[:, 0]


def _row_copy(src_ref, src_row, dst_ref, dst_row, sem):
    return pltpu.make_async_copy(src_ref.at[pl.ds(src_row, 1), :], dst_ref.at[pl.ds(dst_row, 1), :], sem)


def _dispatch_kernel(d0_ref, d1_ref, first_ref, nblk_ref, h_ref, xs_ref, zero_scr, sem, *, tt, n_exp):
    base = pl.program_id(0) * tt
    dests = (d0_ref, d1_ref)

    def pad_copy(e):
        last = pl.multiple_of((first_ref[e] + nblk_ref[e] - 1) * MOE_ROWS, MOE_ROWS)
        return pltpu.make_async_copy(zero_scr, xs_ref.at[pl.ds(last, MOE_ROWS), :], sem)

    def tail_copy(b):
        return pltpu.make_async_copy(
            zero_scr, xs_ref.at[pl.ds(pl.multiple_of(b * MOE_ROWS, MOE_ROWS), MOE_ROWS), :], sem)

    @pl.when(pl.program_id(0) == 0)
    def _():
        zero_scr[...] = jnp.zeros_like(zero_scr)
        for e in range(n_exp):
            @pl.when(nblk_ref[e] > 0)
            def _():
                pad_copy(e).start()
        for e in range(n_exp):
            @pl.when(nblk_ref[e] > 0)
            def _():
                pad_copy(e).wait()
        n_used = first_ref[n_exp - 1] + nblk_ref[n_exp - 1]
        n_blocks = xs_ref.shape[0] // MOE_ROWS
        lax.fori_loop(n_used, n_blocks, lambda b, c: (tail_copy(b).start(), c)[1], 0)
        lax.fori_loop(n_used, n_blocks, lambda b, c: (tail_copy(b).wait(), c)[1], 0)

    def start(t, c):
        for dref in dests:
            _row_copy(h_ref, t, xs_ref, dref[base + t], sem).start()
        return c

    def wait(t, c):
        for dref in dests:
            _row_copy(h_ref, t, xs_ref, dref[base + t], sem).wait()
        return c

    lax.fori_loop(0, tt, start, 0, unroll=DMA_UNROLL)
    lax.fori_loop(0, tt, wait, 0, unroll=DMA_UNROLL)


def _dispatch(h, dest, first, nblk, n_slots, tt=1024):
    n_tok, d = h.shape
    return pl.pallas_call(
        functools.partial(_dispatch_kernel, tt=tt, n_exp=first.shape[0]),
        grid_spec=pltpu.PrefetchScalarGridSpec(
            num_scalar_prefetch=4,
            grid=(n_tok // tt,),
            in_specs=[pl.BlockSpec((tt, d), lambda i, *_: (i, 0))],
            out_specs=pl.BlockSpec(memory_space=pl.ANY),
            scratch_shapes=[pltpu.VMEM((MOE_ROWS, d), h.dtype), pltpu.SemaphoreType.DMA(())]),
        out_shape=jax.ShapeDtypeStruct((n_slots, d), h.dtype),
        compiler_params=_params(("arbitrary",), 40),
        name="moe_dispatch",
    )(dest[0], dest[1], first, nblk, h)


def _expert_kernel(first_ref, nblk_ref, xs_ref, wgu_hbm, wdn_hbm, ys_ref, wgu_stage, wdn_stage, wgu_scr, wdn_scr,
                   xbuf, ybuf, sem_w, sem_x, sem_y, *, layer, hidden):
    e = pl.program_id(0)
    n_exp = pl.num_programs(0)
    nb = nblk_ref[e]
    first = first_ref[e]

    chunks = []
    for hbm, stage, scr in ((wgu_hbm, wgu_stage, wgu_scr), (wdn_hbm, wdn_stage, wdn_scr)):
        rows = stage.shape[0] // W_DMA_CHUNKS
        for c in range(W_DMA_CHUNKS):
            chunks.append((hbm, stage, scr, pl.ds(c * rows, rows), len(chunks)))

    def w_copy(ex, chunk):
        hbm, stage, _, sl, idx = chunk
        return pltpu.make_async_copy(hbm.at[layer, ex, sl, :], stage.at[sl, :], sem_w.at[idx])

    def x_copy(b, slot):
        rows = pl.ds(pl.multiple_of((first + b) * MOE_ROWS, MOE_ROWS), MOE_ROWS)
        return pltpu.make_async_copy(xs_ref.at[rows, :], xbuf.at[slot], sem_x.at[slot])

    def y_copy(b, slot):
        rows = pl.ds(pl.multiple_of((first + b) * MOE_ROWS, MOE_ROWS), MOE_ROWS)
        return pltpu.make_async_copy(ybuf.at[slot], ys_ref.at[rows, :], sem_y.at[slot])

    @pl.when(e == 0)
    def _():
        for chunk in chunks:
            w_copy(0, chunk).start()

    @pl.when(nb > 0)
    def _():
        x_copy(0, 0).start()

    for chunk in chunks:
        _, stage, scr, sl, _ = chunk
        w_copy(e, chunk).wait()
        scr[sl, :] = stage[sl, :].astype(BF16)

        @pl.when(e + 1 < n_exp)
        def _():
            w_copy(e + 1, chunk).start()

    def block(b, carry):
        slot = b & 1
        x_copy(b, slot).wait()

        @pl.when(b + 1 < nb)
        def _():
            x_copy(b + 1, 1 - slot).start()

        @pl.when(b >= 2)
        def _():
            y_copy(b - 2, slot).wait()

        gu = jnp.dot(_unpack_bf16(xbuf[slot]).astype(BF16), wgu_scr[...], preferred_element_type=F32)
        g = gu[:, :hidden]
        act = (g * _sigmoid(g)) * gu[:, hidden:]
        ybuf[slot] = _pack_bf16(jnp.dot(act.astype(BF16), wdn_scr[...], preferred_element_type=F32))
        y_copy(b, slot).start()
        return carry

    lax.fori_loop(0, nb, block, 0)

    @pl.when(nb >= 2)
    def _():
        y_copy(nb - 2, nb & 1).wait()

    @pl.when(nb >= 1)
    def _():
        y_copy(nb - 1, (nb - 1) & 1).wait()

    @pl.when(e == n_exp - 1)
    def _():
        ybuf[0] = jnp.zeros(ybuf.shape[1:], ybuf.dtype)
        n_blocks = ys_ref.shape[0] // MOE_ROWS
        lax.fori_loop(nb, n_blocks - first, lambda b, c: (y_copy(b, 0).start(), c)[1], 0)
        lax.fori_loop(nb, n_blocks - first, lambda b, c: (y_copy(b, 0).wait(), c)[1], 0)


def _experts(xs, first, nblk, layer, w_gu, w_down):
    n_exp, d, two_f = w_gu.shape[1], w_gu.shape[2], w_gu.shape[3]
    hidden = two_f // 2
    return pl.pallas_call(
        functools.partial(_expert_kernel, layer=layer, hidden=hidden),
        grid_spec=pltpu.PrefetchScalarGridSpec(
            num_scalar_prefetch=2,
            grid=(n_exp,),
            in_specs=[pl.BlockSpec(memory_space=pl.ANY),
                      pl.BlockSpec(memory_space=pl.ANY),
                      pl.BlockSpec(memory_space=pl.ANY)],
            out_specs=pl.BlockSpec(memory_space=pl.ANY),
            scratch_shapes=[pltpu.VMEM((d, two_f), w_gu.dtype), pltpu.VMEM((hidden, d), w_down.dtype),
                            pltpu.VMEM((d, two_f), BF16), pltpu.VMEM((hidden, d), BF16),
                            pltpu.VMEM((2, MOE_ROWS) + xs.shape[1:], xs.dtype),
                            pltpu.VMEM((2, MOE_ROWS) + xs.shape[1:], xs.dtype),
                            pltpu.SemaphoreType.DMA((2 * W_DMA_CHUNKS,)), pltpu.SemaphoreType.DMA((2,)),
                            pltpu.SemaphoreType.DMA((2,))]),
        out_shape=jax.ShapeDtypeStruct(xs.shape, xs.dtype),
        compiler_params=_params(("arbitrary",), 52),
        name="moe_experts",
    )(first, nblk, xs, w_gu, w_down)


def _combine_kernel(d0_ref, d1_ref, x_ref, g_ref, nw_ref, ys_ref, o_ref, buf, sem, *, tt, final_norm):
    base = pl.program_id(0) * tt
    dests = (d0_ref, d1_ref)

    def start(t, c):
        for k, dref in enumerate(dests):
            _row_copy(ys_ref, dref[base + t], buf.at[k], t, sem).start()
        return c

    def wait(t, c):
        for k, dref in enumerate(dests):
            _row_copy(ys_ref, dref[base + t], buf.at[k], t, sem).wait()
        return c

    lax.fori_loop(0, tt, start, 0, unroll=DMA_UNROLL)
    lax.fori_loop(0, tt, wait, 0, unroll=DMA_UNROLL)
    g = g_ref[...]
    out = x_ref[...] + sum(g[:, k:k + 1] * _unpack_bf16(buf[k]) for k in range(MOE_TOPK))
    o_ref[...] = _rms(out, nw_ref[...]) if final_norm else out


def _combine(x2, ys, dest, gates_t, final_w, tt=256):
    t, d = x2.shape
    nw = jnp.ones((1, d), F32) if final_w is None else final_w.reshape(1, d).astype(F32)
    return pl.pallas_call(
        functools.partial(_combine_kernel, tt=tt, final_norm=final_w is not None),
        grid_spec=pltpu.PrefetchScalarGridSpec(
            num_scalar_prefetch=2,
            grid=(t // tt,),
            in_specs=[pl.BlockSpec((tt, d), lambda i, d0, d1: (i, 0)),
                      pl.BlockSpec((tt, MOE_TOPK), lambda i, d0, d1: (i, 0)),
                      pl.BlockSpec((1, d), lambda i, d0, d1: (0, 0)),
                      pl.BlockSpec(memory_space=pl.ANY)],
            out_specs=pl.BlockSpec((tt, d), lambda i, d0, d1: (i, 0)),
            scratch_shapes=[pltpu.VMEM((MOE_TOPK, tt) + ys.shape[1:], ys.dtype), pltpu.SemaphoreType.DMA(())]),
        out_shape=jax.ShapeDtypeStruct((t, d), F32),
        compiler_params=_params(("arbitrary",), 40),
        name="moe_combine",
    )(dest[0], dest[1], x2, gates_t, nw, ys)


def _hier_moe(x2, layer, nw, w_group, b_group, w_expert, b_expert, w_gu, w_down, final_w):
    t = x2.shape[0]
    n_exp = w_expert.shape[1]
    n_blocks = (t * MOE_TOPK) // MOE_ROWS + n_exp
    h, eidx, gates = _route(x2, nw, w_group, b_group, w_expert, b_expert)
    dest, first, nblk = _meta(eidx, n_exp)
    xs = _dispatch(h, dest, first, nblk, n_blocks * MOE_ROWS)
    ys = _experts(xs, first, nblk, layer, w_gu, w_down)
    return _combine(x2, ys, dest, gates.T, final_w)


def kernel(x, s5_lam_re, s5_lam_im, s5_log_dt, s5_b_re, s5_b_im, s5_c_re, s5_c_im, s5_d, s5_w_glu, ml_w_in, ml_b_gate, ml_g_head, ml_w_out, norm_mix, norm_ffn, moe_w_group, moe_b_group, moe_w_expert, moe_b_expert, moe_w_gu, moe_w_down, norm_final):
    bsz, seq, d = x.shape
    assert norm_mix.shape[0] == 2 and s5_w_glu.shape[0] == 1 and ml_w_in.shape[0] == 1
    x2 = x.reshape(bsz * seq, d).astype(F32)

    ops = _s5_operators(s5_lam_re[0], s5_lam_im[0], s5_log_dt[0], s5_b_re[0], s5_b_im[0],
                        s5_c_re[0], s5_c_im[0], seq // S5_CHUNK)
    g = _s5_mix(x2, _row_scale(x2), norm_mix[0].astype(F32), s5_d[0].astype(F32), ops, bsz, seq)
    x2 = _glu(x2, g, s5_w_glu[0].astype(BF16))
    x2 = _hier_moe(x2, 0, norm_ffn[0], moe_w_group[0], moe_b_group[0], moe_w_expert[0], moe_b_expert[0],
                   moe_w_gu, moe_w_down, None)

    n_heads = ml_b_gate.shape[1] // 2
    dv = ml_w_out.shape[1] // n_heads
    dk = (ml_w_in.shape[2] - 2 * n_heads * dv - 2 * n_heads) // (2 * n_heads)
    n_main = 2 * n_heads * (dk + dv)
    z = _inproj(x2, norm_mix[1], ml_w_in[0, :, :n_main].astype(BF16))
    gcol = _gates(x2, norm_mix[1], ml_w_in[0, :, n_main:].astype(F32), ml_b_gate[0].astype(F32))
    a_bf = _mlstm(z, gcol, gcol.T, ml_g_head[0].astype(F32), bsz, seq, n_heads, dk, dv)
    x2 = _outproj(a_bf, ml_w_out[0].astype(BF16), x2)
    x2 = _hier_moe(x2, 1, norm_ffn[1], moe_w_group[1], moe_b_group[1], moe_w_expert[1], moe_b_expert[1],
                   moe_w_gu, moe_w_down, norm_final)
    return x2.reshape(bsz, seq, d).astype(x.dtype)
```

```python
import functools
import math

import jax
import jax.numpy as jnp
from jax import lax
from jax.experimental import pallas as pl
from jax.experimental.pallas import tpu as pltpu

F32 = jnp.float32
BF16 = jnp.bfloat16
I32 = jnp.int32

EPS = 1e-6
GATE_SOFTCAP = 15.0
MOE_TOPK = 2

LANES = 128
S5_CHUNK = 16
S5_OUT_STEPS = 2
S5_BATCH = 2
ML_CHUNK = 256
ML_HEAD_GROUP = 4
MOE_ROWS_LOG2 = 8
MOE_ROWS = 1 << MOE_ROWS_LOG2
VMEM_BYTES = 64 * 1024 * 1024
DMA_UNROLL = 8
W_DMA_CHUNKS = 4
DMA_THREADS = 2
W_DMA_THREAD = 1
HIGHEST = lax.Precision.HIGHEST


def _params(sem, vmem_mib):
    assert vmem_mib * 1024 * 1024 < VMEM_BYTES
    return pltpu.CompilerParams(dimension_semantics=sem, vmem_limit_bytes=vmem_mib * 1024 * 1024)


def _rms(x32, w):
    ms = jnp.mean(x32 * x32, axis=-1, keepdims=True)
    return x32 * lax.rsqrt(ms + EPS) * w


def _sigmoid(x):
    return 1.0 / (1.0 + jnp.exp(-x))


def _gelu_tanh(x):
    c = math.sqrt(2.0 / math.pi)
    return 0.5 * x * (1.0 + jnp.tanh(c * (x + 0.044715 * (x * x * x))))


def _split_bf16(a32):
    hi = a32.astype(BF16)
    return hi, (a32 - hi.astype(F32)).astype(BF16)


def _dot3(a_hi, a_lo, b_hi, b_lo, dot):
    return (dot(a_hi, b_hi, preferred_element_type=F32) + dot(a_lo, b_hi, preferred_element_type=F32)
            + dot(a_hi, b_lo, preferred_element_type=F32))


def _dot_nt(a, b, preferred_element_type):
    return lax.dot_general(a, b, (((1,), (1,)), ((), ())), preferred_element_type=preferred_element_type)


HI16 = -65536
LO16 = 65535


def _pack_bf16(v):
    half = v.shape[1] // 2
    bits = lax.bitcast_convert_type(v.astype(BF16).astype(F32), I32)
    return (bits[:, half:] & HI16) | ((bits[:, :half] >> 16) & LO16)


def _unpack_bf16(w):
    lo = lax.bitcast_convert_type(w << 16, F32)
    hi = lax.bitcast_convert_type(w & HI16, F32)
    return jnp.concatenate([lo, hi], axis=1)


def _row_scale_kernel(x_ref, r_ref):
    x = x_ref[...]
    r_ref[...] = lax.rsqrt(jnp.mean(x * x, axis=-1, keepdims=True) + EPS)


def _row_scale(x2, tm=512):
    t, d = x2.shape
    return pl.pallas_call(
        _row_scale_kernel,
        grid=(t // tm,),
        in_specs=[pl.BlockSpec((tm, d), lambda i: (i, 0))],
        out_specs=pl.BlockSpec((tm, 1), lambda i: (i, 0)),
        out_shape=jax.ShapeDtypeStruct((t, 1), F32),
        compiler_params=_params(("parallel",), 32),
        name="row_scale",
    )(x2)


def _s5_build_operators(bexp_ref, cexp_ref, qpow_ref, ppow_ref, win_scr, wout_scr, toep_scr):
    L = qpow_ref.shape[1]
    half = bexp_ref.shape[2] // 2

    def tile(ref, s, t):
        return ref.at[s * LANES:(s + 1) * LANES, t * LANES:(t + 1) * LANES]

    br, bi = bexp_ref[0, :, :half], bexp_ref[0, :, half:]
    cr, ci = cexp_ref[0, :half, :], cexp_ref[0, half:, :]
    b_cat = bexp_ref[0].astype(BF16)
    for s in range(L):
        qr, qi = qpow_ref[0, s:s + 1, :half], qpow_ref[0, s:s + 1, half:]
        win_scr[s * LANES:(s + 1) * LANES, :] = jnp.concatenate(
            [br * qr - bi * qi, br * qi + bi * qr], axis=1).astype(BF16)
    for t in range(L + 1):
        pr, pi = ppow_ref[0, :half, t:t + 1], ppow_ref[0, half:, t:t + 1]
        w_t = jnp.concatenate([cr * pr - ci * pi, -(cr * pi + ci * pr)], axis=0).astype(BF16)
        if t >= 1:
            wout_scr[:, (t - 1) * LANES:t * LANES] = w_t
        if t < L:
            tap = jnp.dot(b_cat, w_t, preferred_element_type=F32).astype(BF16)
            for s in range(L - t):
                tile(toep_scr, s, s + t)[...] = tap
    zero = jnp.zeros((LANES, LANES), BF16)
    for s in range(L):
        for tau in range(s):
            tile(toep_scr, s, tau)[...] = zero


def _s5_kernel(x_ref, r_ref, nw_ref, dsk_ref, bexp_ref, cexp_ref, qpow_ref, ppow_ref, lpow_ref, g_ref,
               win_scr, wout_scr, toep_scr, u_scr, *, n_chunks):
    L = qpow_ref.shape[1]
    rows = u_scr.shape[0]

    @pl.when(pl.program_id(1) == 0)
    def _():
        _s5_build_operators(bexp_ref, cexp_ref, qpow_ref, ppow_ref, win_scr, wout_scr, toep_scr)

    def h_step(s):
        return x_ref[pl.ds(s, rows, stride=L), :] * r_ref[:, s:s + 1] * nw_ref[...]

    for s in range(L):
        u_scr[:, s * LANES:(s + 1) * LANES] = h_step(s).astype(BF16)
    v = jnp.dot(u_scr[...], win_scr[...], preferred_element_type=F32)
    half = v.shape[1] // 2
    assert n_chunks & (n_chunks - 1) == 0 and rows % n_chunks == 0
    cidx = lax.broadcasted_iota(I32, (rows, 1), 0) & (n_chunks - 1)

    def shifted(a, d):
        return jnp.where(cidx >= d, pltpu.roll(a, d, 0), 0.0)

    xr = shifted(v[:, :half], 1)
    xi = shifted(v[:, half:], 1)
    d, k = 1, 0
    while d < n_chunks:
        ar = lpow_ref[0, k:k + 1, :half]
        ai = lpow_ref[0, k:k + 1, half:]
        sr = shifted(xr, d)
        si = shifted(xi, d)
        xr, xi = xr + (ar * sr - ai * si), xi + (ar * si + ai * sr)
        d, k = 2 * d, k + 1
    state = jnp.concatenate([xr, xi], axis=1).astype(BF16)
    for j in range(L // S5_OUT_STEPS):
        lo, hi = j * S5_OUT_STEPS * LANES, (j + 1) * S5_OUT_STEPS * LANES
        y = (jnp.dot(u_scr[:, :hi], toep_scr[:hi, lo:hi], preferred_element_type=F32)
             + jnp.dot(state, wout_scr[:, lo:hi], preferred_element_type=F32))
        for k in range(S5_OUT_STEPS):
            tau = j * S5_OUT_STEPS + k
            g_ref[pl.ds(tau, rows, stride=L), :] = _gelu_tanh(
                y[:, k * LANES:(k + 1) * LANES] + dsk_ref[...] * h_step(tau))


def _s5_operators(lam_re, lam_im, log_dt, b_re, b_im, c_re, c_im, n_chunks):
    g, p, cg = b_re.shape
    L = S5_CHUNK
    gt = LANES // cg
    nt = g // gt
    lam = lax.complex(lam_re.astype(F32), lam_im.astype(F32))
    dt = jnp.exp(log_dt.astype(F32))[:, None]
    z = lam * dt
    lam_bar = jnp.exp(z)
    b_bar = ((lam_bar - 1.0) / lam)[..., None] * lax.complex(b_re.astype(F32), b_im.astype(F32))
    cc = lax.complex(c_re.astype(F32), c_im.astype(F32))
    steps = jnp.arange(L + 1, dtype=F32)
    pw = jnp.exp(z[:, None, :] * steps[None, :, None])
    eye = jnp.eye(gt, dtype=F32)

    def re_im(a, axis):
        return jnp.stack([jnp.real(a), jnp.imag(a)], axis=axis)

    def per_tile(a):
        k = a.shape[1]
        return jnp.transpose(a.reshape(nt, gt, k, p), (0, 2, 1, 3)).reshape(nt, k, gt * p)

    bexp = re_im(jnp.transpose(b_bar, (0, 2, 1)), 2).reshape(nt, gt, cg, 2, p)
    bexp = jnp.einsum('iacrp,ab->iacrbp', bexp, eye).reshape(nt, gt * cg, 2 * gt * p)
    cexp = re_im(jnp.transpose(cc, (0, 2, 1)), 1).reshape(nt, gt, 2, p, cg)
    cexp = jnp.einsum('iarpc,ab->irapbc', cexp, eye).reshape(nt, 2 * gt * p, gt * cg)
    q = pw[:, L - 1 - jnp.arange(L), :]
    qpow = jnp.concatenate([per_tile(jnp.real(q)), per_tile(jnp.imag(q))], axis=-1)
    pt = jnp.concatenate([per_tile(jnp.real(pw)), per_tile(jnp.imag(pw))], axis=-1)
    ppow = jnp.zeros((nt, 2 * gt * p, LANES), F32).at[:, :, :L + 1].set(jnp.transpose(pt, (0, 2, 1)))

    n_steps = max(1, (n_chunks - 1).bit_length())
    hops = (2.0 ** jnp.arange(n_steps, dtype=F32)) * L
    lp = jnp.exp(z[:, None, :] * hops[None, :, None])
    lpow = jnp.concatenate([per_tile(jnp.real(lp)), per_tile(jnp.imag(lp))], axis=-1)
    return bexp, cexp, qpow, ppow, lpow


def _s5_mix(x2, r, nw, d_skip, ops, bsz, seq):
    t, d = x2.shape
    L = S5_CHUNK
    n_chunks = seq // L
    nt = d // LANES
    n_state = ops[0].shape[2]
    r2 = r.reshape(t // L, L)
    bb = S5_BATCH if bsz % S5_BATCH == 0 else 1
    rows = bb * n_chunks
    return pl.pallas_call(
        functools.partial(_s5_kernel, n_chunks=n_chunks),
        grid=(nt, bsz // bb),
        in_specs=[pl.BlockSpec((bb * seq, LANES), lambda i, b: (b, i)),
                  pl.BlockSpec((rows, L), lambda i, b: (b, 0)),
                  pl.BlockSpec((1, LANES), lambda i, b: (0, i)),
                  pl.BlockSpec((1, LANES), lambda i, b: (0, i))]
                 + [pl.BlockSpec((1,) + a.shape[1:], lambda i, b: (i, 0, 0)) for a in ops],
        out_specs=pl.BlockSpec((bb * seq, LANES), lambda i, b: (b, i)),
        out_shape=jax.ShapeDtypeStruct((t, d), F32),
        scratch_shapes=[pltpu.VMEM((L * LANES, n_state), BF16), pltpu.VMEM((n_state, L * LANES), BF16),
                        pltpu.VMEM((L * LANES, L * LANES), BF16), pltpu.VMEM((rows, L * LANES), BF16)],
        compiler_params=_params(("arbitrary", "arbitrary"), 56),
        name="s5_scan",
    )(x2, r2, nw.reshape(1, d), d_skip.reshape(1, d), *ops)


def _glu_kernel(g_ref, wv_ref, wg_ref, xres_ref, o_ref, g_scr):
    @pl.when(pl.program_id(1) == 0)
    def _():
        g_scr[...] = g_ref[...].astype(BF16)

    g = g_scr[...]
    val = jnp.dot(g, wv_ref[...], preferred_element_type=F32)
    gate = jnp.dot(g, wg_ref[...], preferred_element_type=F32)
    o_ref[...] = xres_ref[...] + val * _sigmoid(gate)


def _glu(x2, g, w_glu_bf, tm=1024, tn=512):
    t, d = x2.shape
    nj = d // tn
    return pl.pallas_call(
        _glu_kernel,
        grid=(t // tm, nj),
        in_specs=[pl.BlockSpec((tm, d), lambda i, j: (i, 0)),
                  pl.BlockSpec((d, tn), lambda i, j: (0, j)),
                  pl.BlockSpec((d, tn), lambda i, j: (0, j + nj)),
                  pl.BlockSpec((tm, tn), lambda i, j: (i, j))],
        out_specs=pl.BlockSpec((tm, tn), lambda i, j: (i, j)),
        out_shape=jax.ShapeDtypeStruct((t, d), F32),
        scratch_shapes=[pltpu.VMEM((tm, d), BF16)],
        compiler_params=_params(("parallel", "arbitrary"), 48),
        name="s5_glu",
    )(g, w_glu_bf, w_glu_bf, x2)


def _inproj_kernel(x_ref, nw_ref, w_ref, wg_hi_ref, wg_lo_ref, bg_ref, z_ref, g_ref, h_scr, *, n_heads):
    @pl.when(pl.program_id(1) == 0)
    def _():
        h_hi, h_lo = _split_bf16(_rms(x_ref[...], nw_ref[...]))
        h_scr[...] = h_hi
        pre = _dot3(h_hi, h_lo, wg_hi_ref[...], wg_lo_ref[...], jnp.dot) + bg_ref[...]
        t = GATE_SOFTCAP * jnp.tanh(pre / GATE_SOFTCAP)
        log_sig = jnp.minimum(t, 0.0) - jnp.log(1.0 + jnp.exp(-jnp.abs(t)))
        g_ref[...] = jnp.where(lax.broadcasted_iota(I32, pre.shape, 1) >= n_heads, log_sig, t)

    z_ref[...] = jnp.dot(h_scr[...], w_ref[...], preferred_element_type=F32).astype(z_ref.dtype)


def _inproj(x2, nw, w_bf, w_gate, b_gate, tm=1024, tn=1024):
    t, d = x2.shape
    n = w_bf.shape[1]
    ng = w_gate.shape[1]
    wg_hi, wg_lo = _split_bf16(w_gate.astype(F32))
    return pl.pallas_call(
        functools.partial(_inproj_kernel, n_heads=ng // 2),
        grid=(t // tm, n // tn),
        in_specs=[pl.BlockSpec((tm, d), lambda i, j: (i, 0)),
                  pl.BlockSpec((1, d), lambda i, j: (0, 0)),
                  pl.BlockSpec((d, tn), lambda i, j: (0, j)),
                  pl.BlockSpec((d, ng), lambda i, j: (0, 0)),
                  pl.BlockSpec((d, ng), lambda i, j: (0, 0)),
                  pl.BlockSpec((1, ng), lambda i, j: (0, 0))],
        out_specs=[pl.BlockSpec((tm, tn), lambda i, j: (i, j)), pl.BlockSpec((tm, ng), lambda i, j: (i, 0))],
        out_shape=[jax.ShapeDtypeStruct((t, n), BF16), jax.ShapeDtypeStruct((t, ng), F32)],
        scratch_shapes=[pltpu.VMEM((tm, d), BF16)],
        compiler_params=_params(("parallel", "arbitrary"), 48),
        name="ml_inproj",
    )(x2, nw.reshape(1, d), w_bf, wg_hi, wg_lo, b_gate.reshape(1, ng).astype(F32))


def _mlstm_head(head, q, k, v, o, gc, gr, g_head, c_ref, n_ref, m_ref, n_heads, scale):
    lane_h = lax.broadcasted_iota(I32, gc.shape, 1)
    sub_h = lax.broadcasted_iota(I32, gr.shape, 0)
    i_col = jnp.sum(jnp.where(lane_h == head, gc, 0.0), axis=1, keepdims=True)
    f_col = jnp.sum(jnp.where(lane_h == head + n_heads, gc, 0.0), axis=1, keepdims=True)
    i_row = jnp.sum(jnp.where(sub_h == head, gr, 0.0), axis=0, keepdims=True)
    f_row = jnp.sum(jnp.where(sub_h == head + n_heads, gr, 0.0), axis=0, keepdims=True)

    L = gc.shape[0]
    t_idx = lax.broadcasted_iota(I32, (L, L), 0)
    s_idx = lax.broadcasted_iota(I32, (L, L), 1)
    causal = s_idx <= t_idx
    bcum_col = jnp.sum(jnp.where(causal, f_row, 0.0), axis=1, keepdims=True)
    bcum_row = jnp.sum(jnp.where(t_idx <= s_idx, f_col, 0.0), axis=0, keepdims=True)

    m_prev = m_ref[0:1, 0:1]
    a_log = bcum_col + m_prev
    d_log = jnp.where(causal, bcum_col - bcum_row + i_row, -jnp.inf)
    m_t = jnp.maximum(a_log, jnp.max(d_log, axis=1, keepdims=True))
    w_intra = jnp.exp(d_log - m_t)
    w_inter = jnp.exp(a_log - m_t)

    q32 = q.astype(F32)
    k32 = k.astype(F32)
    c_mat = c_ref[...]
    n_row = n_ref[...]
    sc = lax.dot_general(q, k, (((1,), (1,)), ((), ())), preferred_element_type=F32) * (w_intra * scale)
    num = (w_inter * jnp.dot(q, c_mat.astype(BF16), preferred_element_type=F32)
           + jnp.dot(sc.astype(BF16), v, preferred_element_type=F32))
    den = (w_inter * jnp.sum(q32 * n_row, axis=1, keepdims=True) + jnp.sum(sc, axis=1, keepdims=True))
    h_out = num / jnp.maximum(jnp.abs(den), jnp.exp(-m_t))

    b_last = bcum_col[L - 1:L, :]
    g_log = b_last - bcum_col + i_col
    m_new = jnp.maximum(b_last + m_prev, jnp.max(g_log, axis=0, keepdims=True))
    decay = jnp.exp(b_last + m_prev - m_new)
    wk = jnp.exp(g_log - m_new) * scale
    wv = (wk * v.astype(F32)).astype(BF16)
    kv = lax.dot_general(k, wv, (((0,), (0,)), ((), ())), preferred_element_type=F32)
    c_ref[...] = decay * c_mat + kv
    n_ref[...] = decay * n_row + jnp.sum(wk * k32, axis=0, keepdims=True)
    m_ref[...] = jnp.broadcast_to(m_new, m_ref.shape)

    hn = h_out * lax.rsqrt(jnp.mean(h_out * h_out, axis=-1, keepdims=True) + EPS) * g_head
    return _sigmoid(o.astype(F32)) * hn


def _mlstm_kernel(q_ref, k_ref, v_ref, o_ref, gc_ref, gr_ref, gh_ref, out_ref, c_scr, n_scr, m_scr,
                  *, n_heads, scale):
    @pl.when(pl.program_id(2) == 0)
    def _():
        c_scr[...] = jnp.zeros_like(c_scr)
        n_scr[...] = jnp.zeros_like(n_scr)
        m_scr[...] = jnp.zeros_like(m_scr)

    group, dk, dv = c_scr.shape
    gc = gc_ref[...]
    gr = gr_ref[...]
    for j in range(group):
        qk_cols = slice(j * dk, (j + 1) * dk)
        v_cols = slice(j * dv, (j + 1) * dv)
        out = _mlstm_head(pl.program_id(1) * group + j, q_ref[:, qk_cols], k_ref[:, qk_cols], v_ref[:, v_cols],
                          o_ref[:, v_cols], gc, gr, gh_ref[:, v_cols], c_scr.at[j], n_scr.at[j], m_scr.at[j],
                          n_heads, scale)
        out_ref[:, v_cols] = out.astype(out_ref.dtype)


def _mlstm(z, gcol, grow, g_head, bsz, seq, n_heads, dk, dv):
    t = bsz * seq
    L = ML_CHUNK
    nc = seq // L
    hp = ML_HEAD_GROUP if n_heads % ML_HEAD_GROUP == 0 else 1
    qk_w = n_heads * dk
    ng = 2 * n_heads
    k_off = qk_w // (hp * dk)
    v_off = (2 * qk_w) // (hp * dv)
    o_off = (2 * qk_w + n_heads * dv) // (hp * dv)
    return pl.pallas_call(
        functools.partial(_mlstm_kernel, n_heads=n_heads, scale=dk ** -0.5),
        grid=(bsz, n_heads // hp, nc),
        in_specs=[pl.BlockSpec((L, hp * dk), lambda b, h, c: (b * nc + c, h)),
                  pl.BlockSpec((L, hp * dk), lambda b, h, c: (b * nc + c, k_off + h)),
                  pl.BlockSpec((L, hp * dv), lambda b, h, c: (b * nc + c, v_off + h)),
                  pl.BlockSpec((L, hp * dv), lambda b, h, c: (b * nc + c, o_off + h)),
                  pl.BlockSpec((L, ng), lambda b, h, c: (b * nc + c, 0)),
                  pl.BlockSpec((ng, L), lambda b, h, c: (0, b * nc + c)),
                  pl.BlockSpec((1, hp * dv), lambda b, h, c: (0, h))],
        out_specs=pl.BlockSpec((L, hp * dv), lambda b, h, c: (b * nc + c, h)),
        out_shape=jax.ShapeDtypeStruct((t, n_heads * dv), BF16),
        scratch_shapes=[pltpu.VMEM((hp, dk, dv), F32), pltpu.VMEM((hp, 1, dk), F32),
                        pltpu.VMEM((hp, 1, LANES), F32)],
        compiler_params=_params(("parallel", "parallel", "arbitrary"), 40),
        name="ml_chunk",
    )(z, z, z, z, gcol, grow, g_head.reshape(1, n_heads * dv))


def _outproj_kernel(a_ref, w_ref, xres_ref, o_ref):
    o_ref[...] = xres_ref[...] + jnp.dot(a_ref[...], w_ref[...], preferred_element_type=F32)


def _outproj(a_bf, w_bf, x2, tm=1024, tn=1024):
    t, kdim = a_bf.shape
    d = w_bf.shape[1]
    return pl.pallas_call(
        _outproj_kernel,
        grid=(t // tm, d // tn),
        in_specs=[pl.BlockSpec((tm, kdim), lambda i, j: (i, 0)),
                  pl.BlockSpec((kdim, tn), lambda i, j: (0, j)),
                  pl.BlockSpec((tm, tn), lambda i, j: (i, j))],
        out_specs=pl.BlockSpec((tm, tn), lambda i, j: (i, j)),
        out_shape=jax.ShapeDtypeStruct((t, d), F32),
        compiler_params=_params(("parallel", "arbitrary"), 40),
        name="ml_outproj",
    )(a_bf, w_bf, x2)


def _route_kernel(x_ref, nw_ref, wt_hi_ref, wt_lo_ref, b_ref, h_ref, eidx_ref, gate_ref, *, n_groups, per_group):
    h_hi, h_lo = _split_bf16(_rms(x_ref[...], nw_ref[...]))
    h_ref[...] = _pack_bf16(h_hi)
    logits = _dot3(wt_hi_ref[...], wt_lo_ref[...], h_hi, h_lo, _dot_nt) + b_ref[...]
    n_exp = n_groups * per_group
    lg = logits[0:n_groups, :]
    le = logits[n_groups:n_groups + n_exp, :]
    g_iota = lax.broadcasted_iota(I32, lg.shape, 0)
    g_max = jnp.max(lg, axis=0, keepdims=True)
    g_sel = jnp.min(jnp.where(lg == g_max, g_iota, n_groups), axis=0, keepdims=True)
    p_sel = 1.0 / jnp.sum(jnp.exp(lg - g_max), axis=0, keepdims=True)

    e_iota = lax.broadcasted_iota(I32, le.shape, 0)
    lo = g_sel * per_group
    cand = jnp.where((e_iota >= lo) & (e_iota < lo + per_group), le, -jnp.inf)
    v1 = jnp.max(cand, axis=0, keepdims=True)
    i1 = jnp.min(jnp.where(cand == v1, e_iota, n_exp), axis=0, keepdims=True)
    cand2 = jnp.where(e_iota == i1, -jnp.inf, cand)
    v2 = jnp.max(cand2, axis=0, keepdims=True)
    i2 = jnp.min(jnp.where(cand2 == v2, e_iota, n_exp), axis=0, keepdims=True)
    e2 = jnp.exp(v2 - v1)
    inv = p_sel / (1.0 + e2)
    eidx_ref[...] = jnp.concatenate([i1, i2], axis=0)
    gate_ref[...] = jnp.concatenate([inv, e2 * inv], axis=0)


def _route(x2, nw, w_group, b_group, w_expert, b_expert, tm=512):
    t, d = x2.shape
    n_groups = w_group.shape[1]
    n_exp = w_expert.shape[1]
    n_rows = -(-(n_groups + n_exp) // LANES) * LANES
    wt = jnp.zeros((n_rows, d), F32).at[:n_groups].set(w_group.T.astype(F32))
    wt = wt.at[n_groups:n_groups + n_exp].set(w_expert.T.astype(F32))
    bias = jnp.zeros((n_rows, 1), F32).at[:n_groups, 0].set(b_group.astype(F32))
    bias = bias.at[n_groups:n_groups + n_exp, 0].set(b_expert.astype(F32))
    return pl.pallas_call(
        functools.partial(_route_kernel, n_groups=n_groups, per_group=n_exp // n_groups),
        grid=(t // tm,),
        in_specs=[pl.BlockSpec((tm, d), lambda i: (i, 0)),
                  pl.BlockSpec((1, d), lambda i: (0, 0)),
                  pl.BlockSpec((n_rows, d), lambda i: (0, 0)),
                  pl.BlockSpec((n_rows, d), lambda i: (0, 0)),
                  pl.BlockSpec((n_rows, 1), lambda i: (0, 0))],
        out_specs=[pl.BlockSpec((tm, d // 2), lambda i: (i, 0)),
                   pl.BlockSpec((MOE_TOPK, tm), lambda i: (0, i)),
                   pl.BlockSpec((MOE_TOPK, tm), lambda i: (0, i))],
        out_shape=[jax.ShapeDtypeStruct((t, d // 2), I32),
                   jax.ShapeDtypeStruct((MOE_TOPK, t), I32),
                   jax.ShapeDtypeStruct((MOE_TOPK, t), F32)],
        compiler_params=_params(("parallel",), 40),
        name="moe_route",
    )(x2, nw.reshape(1, d), *_split_bf16(wt), bias)


def _meta_kernel(eidx_ref, dest_ref, first_ref, nblk_ref, *, n_exp, n_tiles):
    tl = eidx_ref.shape[1]
    e_iota = lax.broadcasted_iota(I32, (n_exp, tl), 0)
    upper = (lax.broadcasted_iota(I32, (tl, tl), 0) < lax.broadcasted_iota(I32, (tl, tl), 1)).astype(BF16)

    def hits(i):
        return [e_iota == eidx_ref[pl.ds(k * n_tiles + i, 1), :] for k in range(MOE_TOPK)]

    def rank_tile(i, carry):
        hit = hits(i)
        member = sum(h.astype(F32) for h in hit)
        before = jnp.dot(member.astype(BF16), upper, preferred_element_type=F32) + carry
        for k in range(MOE_TOPK):
            rank = jnp.sum(jnp.where(hit[k], before, 0.0), axis=0, keepdims=True)
            dest_ref[pl.ds(k * n_tiles + i, 1), :] = rank.astype(I32)
        return carry + jnp.sum(member, axis=1, keepdims=True)

    counts = lax.fori_loop(0, n_tiles, rank_tile, jnp.zeros((n_exp, 1), F32)).astype(I32)
    nblk = (counts + (MOE_ROWS - 1)) >> MOE_ROWS_LOG2
    lower = (lax.broadcasted_iota(I32, (n_exp, n_exp), 1) < lax.broadcasted_iota(I32, (n_exp, n_exp), 0))
    first = jnp.dot(lower.astype(F32), jnp.broadcast_to(nblk.astype(F32), (n_exp, LANES)),
                    preferred_element_type=F32, precision=HIGHEST)[:, 0:1].astype(I32)
    pad_start = first << MOE_ROWS_LOG2

    def dest_tile(i, carry):
        hit = hits(i)
        for k in range(MOE_TOPK):
            base = jnp.sum(jnp.where(hit[k], pad_start, 0), axis=0, keepdims=True)
            row = pl.ds(k * n_tiles + i, 1)
            dest_ref[row, :] = dest_ref[row, :] + base
        return carry

    lax.fori_loop(0, n_tiles, dest_tile, 0)
    first_ref[...] = first
    nblk_ref[...] = nblk


def _meta(eidx, n_exp, tl=512):
    n_tok = eidx.shape[1]
    n_tiles = n_tok // tl
    dest, first, nblk = pl.pallas_call(
        functools.partial(_meta_kernel, n_exp=n_exp, n_tiles=n_tiles),
        out_shape=[jax.ShapeDtypeStruct((MOE_TOPK * n_tiles, tl), I32),
                   jax.ShapeDtypeStruct((n_exp, 1), I32),
                   jax.ShapeDtypeStruct((n_exp, 1), I32)],
        compiler_params=pltpu.CompilerParams(vmem_limit_bytes=32 * 1024 * 1024),
        name="moe_meta",
    )(eidx.reshape(MOE_TOPK * n_tiles, tl))
    return dest.reshape(MOE_TOPK, n_tok), first[:, 0], nblk[:, 0]


def _row_copy(src_ref, src_row, dst_ref, dst_row, sem):
    return pltpu.make_async_copy(src_ref.at[pl.ds(src_row, 1), :], dst_ref.at[pl.ds(dst_row, 1), :], sem)


def _dispatch_kernel(d0_ref, d1_ref, first_ref, nblk_ref, h_ref, xs_ref, zero_scr, sem, *, tt, n_exp):
    base = pl.program_id(0) * tt
    dests = (d0_ref, d1_ref)

    def pad_copy(e):
        last = pl.multiple_of((first_ref[e] + nblk_ref[e] - 1) * MOE_ROWS, MOE_ROWS)
        return pltpu.make_async_copy(zero_scr, xs_ref.at[pl.ds(last, MOE_ROWS), :], sem)

    def tail_copy(b):
        return pltpu.make_async_copy(
            zero_scr, xs_ref.at[pl.ds(pl.multiple_of(b * MOE_ROWS, MOE_ROWS), MOE_ROWS), :], sem)

    @pl.when(pl.program_id(0) == 0)
    def _():
        zero_scr[...] = jnp.zeros_like(zero_scr)
        for e in range(n_exp):
            @pl.when(nblk_ref[e] > 0)
            def _():
                pad_copy(e).start()
        for e in range(n_exp):
            @pl.when(nblk_ref[e] > 0)
            def _():
                pad_copy(e).wait()
        n_used = first_ref[n_exp - 1] + nblk_ref[n_exp - 1]
        n_blocks = xs_ref.shape[0] // MOE_ROWS
        lax.fori_loop(n_used, n_blocks, lambda b, c: (tail_copy(b).start(), c)[1], 0)
        lax.fori_loop(n_used, n_blocks, lambda b, c: (tail_copy(b).wait(), c)[1], 0)

    def start(t, c):
        for k, dref in enumerate(dests):
            _row_copy(h_ref, t, xs_ref, dref[base + t], sem).start(priority=k % DMA_THREADS)
        return c

    def wait(t, c):
        for dref in dests:
            _row_copy(h_ref, t, xs_ref, dref[base + t], sem).wait()
        return c

    lax.fori_loop(0, tt, start, 0, unroll=DMA_UNROLL)
    lax.fori_loop(0, tt, wait, 0, unroll=DMA_UNROLL)


def _dispatch(h, dest, first, nblk, n_slots, tt=1024):
    n_tok, d = h.shape
    return pl.pallas_call(
        functools.partial(_dispatch_kernel, tt=tt, n_exp=first.shape[0]),
        grid_spec=pltpu.PrefetchScalarGridSpec(
            num_scalar_prefetch=4,
            grid=(n_tok // tt,),
            in_specs=[pl.BlockSpec((tt, d), lambda i, *_: (i, 0))],
            out_specs=pl.BlockSpec(memory_space=pl.ANY),
            scratch_shapes=[pltpu.VMEM((MOE_ROWS, d), h.dtype), pltpu.SemaphoreType.DMA(())]),
        out_shape=jax.ShapeDtypeStruct((n_slots, d), h.dtype),
        compiler_params=_params(("arbitrary",), 40),
        name="moe_dispatch",
    )(dest[0], dest[1], first, nblk, h)


def _expert_kernel(first_ref, nblk_ref, xs_ref, wgu_hbm, wdn_hbm, ys_ref, wgu_stage, wdn_stage, wgu_scr, wdn_scr,
                   xbuf, ybuf, sem_w, sem_x, sem_y, *, layer, hidden):
    e = pl.program_id(0)
    n_exp = pl.num_programs(0)
    nb = nblk_ref[e]
    first = first_ref[e]

    chunks = []
    for hbm, stage, scr in ((wgu_hbm, wgu_stage, wgu_scr), (wdn_hbm, wdn_stage, wdn_scr)):
        rows = stage.shape[0] // W_DMA_CHUNKS
        for c in range(W_DMA_CHUNKS):
            chunks.append((hbm, stage, scr, pl.ds(c * rows, rows), len(chunks)))

    def w_copy(ex, chunk):
        hbm, stage, _, sl, idx = chunk
        return pltpu.make_async_copy(hbm.at[layer, ex, sl, :], stage.at[sl, :], sem_w.at[idx])

    def x_copy(b, slot):
        rows = pl.ds(pl.multiple_of((first + b) * MOE_ROWS, MOE_ROWS), MOE_ROWS)
        return pltpu.make_async_copy(xs_ref.at[rows, :], xbuf.at[slot], sem_x.at[slot])

    def y_copy(b, slot):
        rows = pl.ds(pl.multiple_of((first + b) * MOE_ROWS, MOE_ROWS), MOE_ROWS)
        return pltpu.make_async_copy(ybuf.at[slot], ys_ref.at[rows, :], sem_y.at[slot])

    @pl.when(e == 0)
    def _():
        for chunk in chunks:
            w_copy(0, chunk).start(priority=W_DMA_THREAD)

    @pl.when(nb > 0)
    def _():
        x_copy(0, 0).start()

    for chunk in chunks:
        _, stage, scr, sl, _ = chunk
        w_copy(e, chunk).wait()
        scr[sl, :] = stage[sl, :].astype(BF16)

        @pl.when(e + 1 < n_exp)
        def _():
            w_copy(e + 1, chunk).start(priority=W_DMA_THREAD)

    def block(b, carry):
        slot = b & 1
        x_copy(b, slot).wait()

        @pl.when(b + 1 < nb)
        def _():
            x_copy(b + 1, 1 - slot).start()

        @pl.when(b >= 2)
        def _():
            y_copy(b - 2, slot).wait()

        gu = jnp.dot(_unpack_bf16(xbuf[slot]).astype(BF16), wgu_scr[...], preferred_element_type=F32)
        g = gu[:, :hidden]
        act = (g * _sigmoid(g)) * gu[:, hidden:]
        ybuf[slot] = _pack_bf16(jnp.dot(act.astype(BF16), wdn_scr[...], preferred_element_type=F32))
        y_copy(b, slot).start()
        return carry

    lax.fori_loop(0, nb, block, 0)

    @pl.when(nb >= 2)
    def _():
        y_copy(nb - 2, nb & 1).wait()

    @pl.when(nb >= 1)
    def _():
        y_copy(nb - 1, (nb - 1) & 1).wait()

    @pl.when(e == n_exp - 1)
    def _():
        ybuf[0] = jnp.zeros(ybuf.shape[1:], ybuf.dtype)
        n_blocks = ys_ref.shape[0] // MOE_ROWS
        lax.fori_loop(nb, n_blocks - first, lambda b, c: (y_copy(b, 0).start(), c)[1], 0)
        lax.fori_loop(nb, n_blocks - first, lambda b, c: (y_copy(b, 0).wait(), c)[1], 0)


def _experts(xs, first, nblk, layer, w_gu, w_down):
    n_exp, d, two_f = w_gu.shape[1], w_gu.shape[2], w_gu.shape[3]
    hidden = two_f // 2
    return pl.pallas_call(
        functools.partial(_expert_kernel, layer=layer, hidden=hidden),
        grid_spec=pltpu.PrefetchScalarGridSpec(
            num_scalar_prefetch=2,
            grid=(n_exp,),
            in_specs=[pl.BlockSpec(memory_space=pl.ANY),
                      pl.BlockSpec(memory_space=pl.ANY),
                      pl.BlockSpec(memory_space=pl.ANY)],
            out_specs=pl.BlockSpec(memory_space=pl.ANY),
            scratch_shapes=[pltpu.VMEM((d, two_f), w_gu.dtype), pltpu.VMEM((hidden, d), w_down.dtype),
                            pltpu.VMEM((d, two_f), BF16), pltpu.VMEM((hidden, d), BF16),
                            pltpu.VMEM((2, MOE_ROWS) + xs.shape[1:], xs.dtype),
                            pltpu.VMEM((2, MOE_ROWS) + xs.shape[1:], xs.dtype),
                            pltpu.SemaphoreType.DMA((2 * W_DMA_CHUNKS,)), pltpu.SemaphoreType.DMA((2,)),
                            pltpu.SemaphoreType.DMA((2,))]),
        out_shape=jax.ShapeDtypeStruct(xs.shape, xs.dtype),
        compiler_params=_params(("arbitrary",), 52),
        name="moe_experts",
    )(first, nblk, xs, w_gu, w_down)


def _combine_kernel(d0_ref, d1_ref, x_ref, g_ref, nw_ref, ys_ref, o_ref, buf, sem, *, tt, final_norm):
    base = pl.program_id(0) * tt
    dests = (d0_ref, d1_ref)

    def start(t, c):
        for k, dref in enumerate(dests):
            _row_copy(ys_ref, dref[base + t], buf.at[k], t, sem).start(priority=k % DMA_THREADS)
        return c

    def wait(t, c):
        for k, dref in enumerate(dests):
            _row_copy(ys_ref, dref[base + t], buf.at[k], t, sem).wait()
        return c

    lax.fori_loop(0, tt, start, 0, unroll=DMA_UNROLL)
    lax.fori_loop(0, tt, wait, 0, unroll=DMA_UNROLL)
    g = g_ref[...]
    out = x_ref[...] + sum(g[:, k:k + 1] * _unpack_bf16(buf[k]) for k in range(MOE_TOPK))
    o_ref[...] = _rms(out, nw_ref[...]) if final_norm else out


def _combine(x2, ys, dest, gates_t, final_w, tt=256):
    t, d = x2.shape
    nw = jnp.ones((1, d), F32) if final_w is None else final_w.reshape(1, d).astype(F32)
    return pl.pallas_call(
        functools.partial(_combine_kernel, tt=tt, final_norm=final_w is not None),
        grid_spec=pltpu.PrefetchScalarGridSpec(
            num_scalar_prefetch=2,
            grid=(t // tt,),
            in_specs=[pl.BlockSpec((tt, d), lambda i, d0, d1: (i, 0)),
                      pl.BlockSpec((tt, MOE_TOPK), lambda i, d0, d1: (i, 0)),
                      pl.BlockSpec((1, d), lambda i, d0, d1: (0, 0)),
                      pl.BlockSpec(memory_space=pl.ANY)],
            out_specs=pl.BlockSpec((tt, d), lambda i, d0, d1: (i, 0)),
            scratch_shapes=[pltpu.VMEM((MOE_TOPK, tt) + ys.shape[1:], ys.dtype), pltpu.SemaphoreType.DMA(())]),
        out_shape=jax.ShapeDtypeStruct((t, d), F32),
        compiler_params=_params(("arbitrary",), 40),
        name="moe_combine",
    )(dest[0], dest[1], x2, gates_t, nw, ys)


def _hier_moe(x2, layer, nw, w_group, b_group, w_expert, b_expert, w_gu, w_down, final_w):
    t = x2.shape[0]
    n_exp = w_expert.shape[1]
    n_blocks = (t * MOE_TOPK) // MOE_ROWS + n_exp
    h, eidx, gates = _route(x2, nw, w_group, b_group, w_expert, b_expert)
    dest, first, nblk = _meta(eidx, n_exp)
    xs = _dispatch(h, dest, first, nblk, n_blocks * MOE_ROWS)
    ys = _experts(xs, first, nblk, layer, w_gu, w_down)
    return _combine(x2, ys, dest, gates.T, final_w)


def kernel(x, s5_lam_re, s5_lam_im, s5_log_dt, s5_b_re, s5_b_im, s5_c_re, s5_c_im, s5_d, s5_w_glu, ml_w_in, ml_b_gate, ml_g_head, ml_w_out, norm_mix, norm_ffn, moe_w_group, moe_b_group, moe_w_expert, moe_b_expert, moe_w_gu, moe_w_down, norm_final):
    bsz, seq, d = x.shape
    assert norm_mix.shape[0] == 2 and s5_w_glu.shape[0] == 1 and ml_w_in.shape[0] == 1
    x2 = x.reshape(bsz * seq, d).astype(F32)

    ops = _s5_operators(s5_lam_re[0], s5_lam_im[0], s5_log_dt[0], s5_b_re[0], s5_b_im[0],
                        s5_c_re[0], s5_c_im[0], seq // S5_CHUNK)
    g = _s5_mix(x2, _row_scale(x2), norm_mix[0].astype(F32), s5_d[0].astype(F32), ops, bsz, seq)
    x2 = _glu(x2, g, s5_w_glu[0].astype(BF16))
    x2 = _hier_moe(x2, 0, norm_ffn[0], moe_w_group[0], moe_b_group[0], moe_w_expert[0], moe_b_expert[0],
                   moe_w_gu, moe_w_down, None)

    n_heads = ml_b_gate.shape[1] // 2
    dv = ml_w_out.shape[1] // n_heads
    dk = (ml_w_in.shape[2] - 2 * n_heads * dv - 2 * n_heads) // (2 * n_heads)
    n_main = 2 * n_heads * (dk + dv)
    z, gcol = _inproj(x2, norm_mix[1], ml_w_in[0, :, :n_main].astype(BF16), ml_w_in[0, :, n_main:], ml_b_gate[0])
    a_bf = _mlstm(z, gcol, gcol.T, ml_g_head[0].astype(F32), bsz, seq, n_heads, dk, dv)
    x2 = _outproj(a_bf, ml_w_out[0].astype(BF16), x2)
    x2 = _hier_moe(x2, 1, norm_ffn[1], moe_w_group[1], moe_b_group[1], moe_w_expert[1], moe_b_expert[1],
                   moe_w_gu, moe_w_down, norm_final)
    return x2.reshape(bsz, seq, d).astype(x.dtype)
```

```python
import functools
import math

import jax
import jax.numpy as jnp
from jax import lax
from jax.experimental import pallas as pl
from jax.experimental.pallas import tpu as pltpu

F32 = jnp.float32
BF16 = jnp.bfloat16
I32 = jnp.int32

EPS = 1e-6
GATE_SOFTCAP = 15.0
MOE_TOPK = 2

LANES = 128
S5_CHUNK = 16
S5_OUT_STEPS = 2
S5_BATCH = 2
ML_CHUNK = 256
ML_HEAD_GROUP = 4
MOE_ROWS_LOG2 = 8
MOE_ROWS = 1 << MOE_ROWS_LOG2
VMEM_BYTES = 64 * 1024 * 1024
DMA_UNROLL = 8
W_DMA_CHUNKS = 4
DMA_THREADS = 2
W_DMA_THREAD = 1
HIGHEST = lax.Precision.HIGHEST


def _params(sem, vmem_mib):
    assert vmem_mib * 1024 * 1024 < VMEM_BYTES
    return pltpu.CompilerParams(dimension_semantics=sem, vmem_limit_bytes=vmem_mib * 1024 * 1024)


def _rms(x32, w):
    ms = jnp.mean(x32 * x32, axis=-1, keepdims=True)
    return x32 * lax.rsqrt(ms + EPS) * w


def _sigmoid(x):
    return 1.0 / (1.0 + jnp.exp(-x))


def _gelu_tanh(x):
    c = math.sqrt(2.0 / math.pi)
    return 0.5 * x * (1.0 + jnp.tanh(c * (x + 0.044715 * (x * x * x))))


def _split_bf16(a32):
    hi = a32.astype(BF16)
    return hi, (a32 - hi.astype(F32)).astype(BF16)


def _dot3(a_hi, a_lo, b_hi, b_lo, dot):
    return (dot(a_hi, b_hi, preferred_element_type=F32) + dot(a_lo, b_hi, preferred_element_type=F32)
            + dot(a_hi, b_lo, preferred_element_type=F32))


def _dot_nt(a, b, preferred_element_type):
    return lax.dot_general(a, b, (((1,), (1,)), ((), ())), preferred_element_type=preferred_element_type)


HI16 = -65536
LO16 = 65535


def _pack_bf16(v):
    half = v.shape[1] // 2
    bits = lax.bitcast_convert_type(v.astype(BF16).astype(F32), I32)
    return (bits[:, half:] & HI16) | ((bits[:, :half] >> 16) & LO16)


def _unpack_bf16(w):
    lo = lax.bitcast_convert_type(w << 16, F32)
    hi = lax.bitcast_convert_type(w & HI16, F32)
    return jnp.concatenate([lo, hi], axis=1)


def _row_scale_kernel(x_ref, r_ref):
    x = x_ref[...]
    r_ref[...] = lax.rsqrt(jnp.mean(x * x, axis=-1, keepdims=True) + EPS)


def _row_scale(x2, tm=512):
    t, d = x2.shape
    return pl.pallas_call(
        _row_scale_kernel,
        grid=(t // tm,),
        in_specs=[pl.BlockSpec((tm, d), lambda i: (i, 0))],
        out_specs=pl.BlockSpec((tm, 1), lambda i: (i, 0)),
        out_shape=jax.ShapeDtypeStruct((t, 1), F32),
        compiler_params=_params(("parallel",), 32),
        name="row_scale",
    )(x2)


def _s5_build_operators(bexp_ref, cexp_ref, qpow_ref, ppow_ref, win_scr, wout_scr, toep_scr):
    L = qpow_ref.shape[1]
    half = bexp_ref.shape[2] // 2

    def tile(ref, s, t):
        return ref.at[s * LANES:(s + 1) * LANES, t * LANES:(t + 1) * LANES]

    br, bi = bexp_ref[0, :, :half], bexp_ref[0, :, half:]
    cr, ci = cexp_ref[0, :half, :], cexp_ref[0, half:, :]
    b_cat = bexp_ref[0].astype(BF16)
    for s in range(L):
        qr, qi = qpow_ref[0, s:s + 1, :half], qpow_ref[0, s:s + 1, half:]
        win_scr[s * LANES:(s + 1) * LANES, :] = jnp.concatenate(
            [br * qr - bi * qi, br * qi + bi * qr], axis=1).astype(BF16)
    for t in range(L + 1):
        pr, pi = ppow_ref[0, :half, t:t + 1], ppow_ref[0, half:, t:t + 1]
        w_t = jnp.concatenate([cr * pr - ci * pi, -(cr * pi + ci * pr)], axis=0).astype(BF16)
        if t >= 1:
            wout_scr[:, (t - 1) * LANES:t * LANES] = w_t
        if t < L:
            tap = jnp.dot(b_cat, w_t, preferred_element_type=F32).astype(BF16)
            for s in range(L - t):
                tile(toep_scr, s, s + t)[...] = tap
    zero = jnp.zeros((LANES, LANES), BF16)
    for s in range(L):
        for tau in range(s):
            tile(toep_scr, s, tau)[...] = zero


def _s5_kernel(x_ref, r_ref, nw_ref, dsk_ref, bexp_ref, cexp_ref, qpow_ref, ppow_ref, lpow_ref, g_ref,
               win_scr, wout_scr, toep_scr, u_scr, *, n_chunks):
    L = qpow_ref.shape[1]
    rows = u_scr.shape[0]

    @pl.when(pl.program_id(1) == 0)
    def _():
        _s5_build_operators(bexp_ref, cexp_ref, qpow_ref, ppow_ref, win_scr, wout_scr, toep_scr)

    def h_step(s):
        return x_ref[pl.ds(s, rows, stride=L), :] * r_ref[:, s:s + 1] * nw_ref[...]

    for s in range(L):
        u_scr[:, s * LANES:(s + 1) * LANES] = h_step(s).astype(BF16)
    v = jnp.dot(u_scr[...], win_scr[...], preferred_element_type=F32)
    half = v.shape[1] // 2
    assert n_chunks & (n_chunks - 1) == 0 and rows % n_chunks == 0
    cidx = lax.broadcasted_iota(I32, (rows, 1), 0) & (n_chunks - 1)

    def shifted(a, d):
        return jnp.where(cidx >= d, pltpu.roll(a, d, 0), 0.0)

    xr = shifted(v[:, :half], 1)
    xi = shifted(v[:, half:], 1)
    d, k = 1, 0
    while d < n_chunks:
        ar = lpow_ref[0, k:k + 1, :half]
        ai = lpow_ref[0, k:k + 1, half:]
        sr = shifted(xr, d)
        si = shifted(xi, d)
        xr, xi = xr + (ar * sr - ai * si), xi + (ar * si + ai * sr)
        d, k = 2 * d, k + 1
    state = jnp.concatenate([xr, xi], axis=1).astype(BF16)
    for j in range(L // S5_OUT_STEPS):
        lo, hi = j * S5_OUT_STEPS * LANES, (j + 1) * S5_OUT_STEPS * LANES
        y = (jnp.dot(u_scr[:, :hi], toep_scr[:hi, lo:hi], preferred_element_type=F32)
             + jnp.dot(state, wout_scr[:, lo:hi], preferred_element_type=F32))
        for k in range(S5_OUT_STEPS):
            tau = j * S5_OUT_STEPS + k
            g_ref[pl.ds(tau, rows, stride=L), :] = _gelu_tanh(
                y[:, k * LANES:(k + 1) * LANES] + dsk_ref[...] * h_step(tau))


def _s5_operators(lam_re, lam_im, log_dt, b_re, b_im, c_re, c_im, n_chunks):
    g, p, cg = b_re.shape
    L = S5_CHUNK
    gt = LANES // cg
    nt = g // gt
    lam = lax.complex(lam_re.astype(F32), lam_im.astype(F32))
    dt = jnp.exp(log_dt.astype(F32))[:, None]
    z = lam * dt
    lam_bar = jnp.exp(z)
    b_bar = ((lam_bar - 1.0) / lam)[..., None] * lax.complex(b_re.astype(F32), b_im.astype(F32))
    cc = lax.complex(c_re.astype(F32), c_im.astype(F32))
    steps = jnp.arange(L + 1, dtype=F32)
    pw = jnp.exp(z[:, None, :] * steps[None, :, None])
    eye = jnp.eye(gt, dtype=F32)

    def re_im(a, axis):
        return jnp.stack([jnp.real(a), jnp.imag(a)], axis=axis)

    def per_tile(a):
        k = a.shape[1]
        return jnp.transpose(a.reshape(nt, gt, k, p), (0, 2, 1, 3)).reshape(nt, k, gt * p)

    bexp = re_im(jnp.transpose(b_bar, (0, 2, 1)), 2).reshape(nt, gt, cg, 2, p)
    bexp = jnp.einsum('iacrp,ab->iacrbp', bexp, eye).reshape(nt, gt * cg, 2 * gt * p)
    cexp = re_im(jnp.transpose(cc, (0, 2, 1)), 1).reshape(nt, gt, 2, p, cg)
    cexp = jnp.einsum('iarpc,ab->irapbc', cexp, eye).reshape(nt, 2 * gt * p, gt * cg)
    q = pw[:, L - 1 - jnp.arange(L), :]
    qpow = jnp.concatenate([per_tile(jnp.real(q)), per_tile(jnp.imag(q))], axis=-1)
    pt = jnp.concatenate([per_tile(jnp.real(pw)), per_tile(jnp.imag(pw))], axis=-1)
    ppow = jnp.zeros((nt, 2 * gt * p, LANES), F32).at[:, :, :L + 1].set(jnp.transpose(pt, (0, 2, 1)))

    n_steps = max(1, (n_chunks - 1).bit_length())
    hops = (2.0 ** jnp.arange(n_steps, dtype=F32)) * L
    lp = jnp.exp(z[:, None, :] * hops[None, :, None])
    lpow = jnp.concatenate([per_tile(jnp.real(lp)), per_tile(jnp.imag(lp))], axis=-1)
    return bexp, cexp, qpow, ppow, lpow


def _s5_mix(x2, r, nw, d_skip, ops, bsz, seq):
    t, d = x2.shape
    L = S5_CHUNK
    n_chunks = seq // L
    nt = d // LANES
    n_state = ops[0].shape[2]
    r2 = r.reshape(t // L, L)
    bb = S5_BATCH if bsz % S5_BATCH == 0 else 1
    rows = bb * n_chunks
    return pl.pallas_call(
        functools.partial(_s5_kernel, n_chunks=n_chunks),
        grid=(nt, bsz // bb),
        in_specs=[pl.BlockSpec((bb * seq, LANES), lambda i, b: (b, i)),
                  pl.BlockSpec((rows, L), lambda i, b: (b, 0)),
                  pl.BlockSpec((1, LANES), lambda i, b: (0, i)),
                  pl.BlockSpec((1, LANES), lambda i, b: (0, i))]
                 + [pl.BlockSpec((1,) + a.shape[1:], lambda i, b: (i, 0, 0)) for a in ops],
        out_specs=pl.BlockSpec((bb * seq, LANES), lambda i, b: (b, i)),
        out_shape=jax.ShapeDtypeStruct((t, d), F32),
        scratch_shapes=[pltpu.VMEM((L * LANES, n_state), BF16), pltpu.VMEM((n_state, L * LANES), BF16),
                        pltpu.VMEM((L * LANES, L * LANES), BF16), pltpu.VMEM((rows, L * LANES), BF16)],
        compiler_params=_params(("arbitrary", "arbitrary"), 56),
        name="s5_scan",
    )(x2, r2, nw.reshape(1, d), d_skip.reshape(1, d), *ops)


def _glu_kernel(g_ref, wv_ref, wg_ref, xres_ref, o_ref, g_scr):
    @pl.when(pl.program_id(1) == 0)
    def _():
        g_scr[...] = g_ref[...].astype(BF16)

    g = g_scr[...]
    val = jnp.dot(g, wv_ref[...], preferred_element_type=F32)
    gate = jnp.dot(g, wg_ref[...], preferred_element_type=F32)
    o_ref[...] = xres_ref[...] + val * _sigmoid(gate)


def _glu(x2, g, w_glu_bf, tm=1024, tn=512):
    t, d = x2.shape
    nj = d // tn
    return pl.pallas_call(
        _glu_kernel,
        grid=(t // tm, nj),
        in_specs=[pl.BlockSpec((tm, d), lambda i, j: (i, 0)),
                  pl.BlockSpec((d, tn), lambda i, j: (0, j)),
                  pl.BlockSpec((d, tn), lambda i, j: (0, j + nj)),
                  pl.BlockSpec((tm, tn), lambda i, j: (i, j))],
        out_specs=pl.BlockSpec((tm, tn), lambda i, j: (i, j)),
        out_shape=jax.ShapeDtypeStruct((t, d), F32),
        scratch_shapes=[pltpu.VMEM((tm, d), BF16)],
        compiler_params=_params(("parallel", "arbitrary"), 48),
        name="s5_glu",
    )(g, w_glu_bf, w_glu_bf, x2)


def _inproj_kernel(x_ref, nw_ref, w_ref, wg_hi_ref, wg_lo_ref, bg_ref, z_ref, g_ref, h_scr, *, n_heads):
    @pl.when(pl.program_id(1) == 0)
    def _():
        h_hi, h_lo = _split_bf16(_rms(x_ref[...], nw_ref[...]))
        h_scr[...] = h_hi
        pre = _dot3(h_hi, h_lo, wg_hi_ref[...], wg_lo_ref[...], jnp.dot) + bg_ref[...]
        t = GATE_SOFTCAP * jnp.tanh(pre / GATE_SOFTCAP)
        log_sig = jnp.minimum(t, 0.0) - jnp.log(1.0 + jnp.exp(-jnp.abs(t)))
        g_ref[...] = jnp.where(lax.broadcasted_iota(I32, pre.shape, 1) >= n_heads, log_sig, t)

    z_ref[...] = jnp.dot(h_scr[...], w_ref[...], preferred_element_type=F32).astype(z_ref.dtype)


def _inproj(x2, nw, w_bf, w_gate, b_gate, tm=1024, tn=1024):
    t, d = x2.shape
    n = w_bf.shape[1]
    ng = w_gate.shape[1]
    wg_hi, wg_lo = _split_bf16(w_gate.astype(F32))
    return pl.pallas_call(
        functools.partial(_inproj_kernel, n_heads=ng // 2),
        grid=(t // tm, n // tn),
        in_specs=[pl.BlockSpec((tm, d), lambda i, j: (i, 0)),
                  pl.BlockSpec((1, d), lambda i, j: (0, 0)),
                  pl.BlockSpec((d, tn), lambda i, j: (0, j)),
                  pl.BlockSpec((d, ng), lambda i, j: (0, 0)),
                  pl.BlockSpec((d, ng), lambda i, j: (0, 0)),
                  pl.BlockSpec((1, ng), lambda i, j: (0, 0))],
        out_specs=[pl.BlockSpec((tm, tn), lambda i, j: (i, j)), pl.BlockSpec((tm, ng), lambda i, j: (i, 0))],
        out_shape=[jax.ShapeDtypeStruct((t, n), BF16), jax.ShapeDtypeStruct((t, ng), F32)],
        scratch_shapes=[pltpu.VMEM((tm, d), BF16)],
        compiler_params=_params(("parallel", "arbitrary"), 48),
        name="ml_inproj",
    )(x2, nw.reshape(1, d), w_bf, wg_hi, wg_lo, b_gate.reshape(1, ng).astype(F32))


def _mlstm_head(head, q, k, v, o, gc, gr, g_head, c_ref, m_ref, n_heads, scale):
    lane_h = lax.broadcasted_iota(I32, gc.shape, 1)
    sub_h = lax.broadcasted_iota(I32, gr.shape, 0)
    i_col = jnp.sum(jnp.where(lane_h == head, gc, 0.0), axis=1, keepdims=True)
    f_col = jnp.sum(jnp.where(lane_h == head + n_heads, gc, 0.0), axis=1, keepdims=True)
    i_row = jnp.sum(jnp.where(sub_h == head, gr, 0.0), axis=0, keepdims=True)
    f_row = jnp.sum(jnp.where(sub_h == head + n_heads, gr, 0.0), axis=0, keepdims=True)

    L = gc.shape[0]
    t_idx = lax.broadcasted_iota(I32, (L, L), 0)
    s_idx = lax.broadcasted_iota(I32, (L, L), 1)
    causal = s_idx <= t_idx
    bcum_col = jnp.sum(jnp.where(causal, f_row, 0.0), axis=1, keepdims=True)
    bcum_row = jnp.sum(jnp.where(t_idx <= s_idx, f_col, 0.0), axis=0, keepdims=True)

    m_prev = m_ref[0:1, 0:1]
    a_log = bcum_col + m_prev
    d_log = jnp.where(causal, bcum_col - bcum_row + i_row, -jnp.inf)
    m_t = jnp.maximum(a_log, jnp.max(d_log, axis=1, keepdims=True))
    w_intra = jnp.exp(d_log - m_t)
    w_inter = jnp.exp(a_log - m_t)

    dv = v.shape[1]
    ones_col = (lax.broadcasted_iota(I32, (L, LANES), 1) == 0).astype(F32)
    v_ext = jnp.concatenate([v, ones_col.astype(BF16)], axis=1)
    c_ext = c_ref[...]
    sc = lax.dot_general(q, k, (((1,), (1,)), ((), ())), preferred_element_type=F32) * (w_intra * scale)
    num_den = (w_inter * jnp.dot(q, c_ext.astype(BF16), preferred_element_type=F32)
               + jnp.dot(sc.astype(BF16), v_ext, preferred_element_type=F32))
    den = num_den[:, dv:dv + 1]
    h_out = num_den[:, :dv] / jnp.maximum(jnp.abs(den), jnp.exp(-m_t))

    b_last = bcum_col[L - 1:L, :]
    g_log = b_last - bcum_col + i_col
    m_new = jnp.maximum(b_last + m_prev, jnp.max(g_log, axis=0, keepdims=True))
    decay = jnp.exp(b_last + m_prev - m_new)
    wk = jnp.exp(g_log - m_new) * scale
    wv = jnp.concatenate([(wk * v.astype(F32)).astype(BF16), (wk * ones_col).astype(BF16)], axis=1)
    kv = lax.dot_general(k, wv, (((0,), (0,)), ((), ())), preferred_element_type=F32)
    c_ref[...] = decay * c_ext + kv
    m_ref[...] = jnp.broadcast_to(m_new, m_ref.shape)

    hn = h_out * lax.rsqrt(jnp.mean(h_out * h_out, axis=-1, keepdims=True) + EPS) * g_head
    return _sigmoid(o.astype(F32)) * hn


def _mlstm_kernel(q_ref, k_ref, v_ref, o_ref, gc_ref, gr_ref, gh_ref, out_ref, c_scr, m_scr,
                  *, n_heads, scale):
    @pl.when(pl.program_id(2) == 0)
    def _():
        c_scr[...] = jnp.zeros_like(c_scr)
        m_scr[...] = jnp.zeros_like(m_scr)

    group, dk = c_scr.shape[:2]
    dv = c_scr.shape[2] - LANES
    gc = gc_ref[...]
    gr = gr_ref[...]
    for j in range(group):
        qk_cols = slice(j * dk, (j + 1) * dk)
        v_cols = slice(j * dv, (j + 1) * dv)
        out = _mlstm_head(pl.program_id(1) * group + j, q_ref[:, qk_cols], k_ref[:, qk_cols], v_ref[:, v_cols],
                          o_ref[:, v_cols], gc, gr, gh_ref[:, v_cols], c_scr.at[j], m_scr.at[j], n_heads, scale)
        out_ref[:, v_cols] = out.astype(out_ref.dtype)


def _mlstm(z, gcol, grow, g_head, bsz, seq, n_heads, dk, dv):
    t = bsz * seq
    L = ML_CHUNK
    nc = seq // L
    hp = ML_HEAD_GROUP if n_heads % ML_HEAD_GROUP == 0 else 1
    qk_w = n_heads * dk
    ng = 2 * n_heads
    k_off = qk_w // (hp * dk)
    v_off = (2 * qk_w) // (hp * dv)
    o_off = (2 * qk_w + n_heads * dv) // (hp * dv)
    return pl.pallas_call(
        functools.partial(_mlstm_kernel, n_heads=n_heads, scale=dk ** -0.5),
        grid=(bsz, n_heads // hp, nc),
        in_specs=[pl.BlockSpec((L, hp * dk), lambda b, h, c: (b * nc + c, h)),
                  pl.BlockSpec((L, hp * dk), lambda b, h, c: (b * nc + c, k_off + h)),
                  pl.BlockSpec((L, hp * dv), lambda b, h, c: (b * nc + c, v_off + h)),
                  pl.BlockSpec((L, hp * dv), lambda b, h, c: (b * nc + c, o_off + h)),
                  pl.BlockSpec((L, ng), lambda b, h, c: (b * nc + c, 0)),
                  pl.BlockSpec((ng, L), lambda b, h, c: (0, b * nc + c)),
                  pl.BlockSpec((1, hp * dv), lambda b, h, c: (0, h))],
        out_specs=pl.BlockSpec((L, hp * dv), lambda b, h, c: (b * nc + c, h)),
        out_shape=jax.ShapeDtypeStruct((t, n_heads * dv), BF16),
        scratch_shapes=[pltpu.VMEM((hp, dk, dv + LANES), F32), pltpu.VMEM((hp, 1, LANES), F32)],
        compiler_params=_params(("parallel", "parallel", "arbitrary"), 40),
        name="ml_chunk",
    )(z, z, z, z, gcol, grow, g_head.reshape(1, n_heads * dv))


def _outproj_kernel(a_ref, w_ref, xres_ref, o_ref):
    o_ref[...] = xres_ref[...] + jnp.dot(a_ref[...], w_ref[...], preferred_element_type=F32)


def _outproj(a_bf, w_bf, x2, tm=1024, tn=1024):
    t, kdim = a_bf.shape
    d = w_bf.shape[1]
    return pl.pallas_call(
        _outproj_kernel,
        grid=(t // tm, d // tn),
        in_specs=[pl.BlockSpec((tm, kdim), lambda i, j: (i, 0)),
                  pl.BlockSpec((kdim, tn), lambda i, j: (0, j)),
                  pl.BlockSpec((tm, tn), lambda i, j: (i, j))],
        out_specs=pl.BlockSpec((tm, tn), lambda i, j: (i, j)),
        out_shape=jax.ShapeDtypeStruct((t, d), F32),
        compiler_params=_params(("parallel", "arbitrary"), 40),
        name="ml_outproj",
    )(a_bf, w_bf, x2)


def _route_kernel(x_ref, nw_ref, wt_hi_ref, wt_lo_ref, b_ref, h_ref, eidx_ref, gate_ref, *, n_groups, per_group):
    h_hi, h_lo = _split_bf16(_rms(x_ref[...], nw_ref[...]))
    h_ref[...] = _pack_bf16(h_hi)
    logits = _dot3(wt_hi_ref[...], wt_lo_ref[...], h_hi, h_lo, _dot_nt) + b_ref[...]
    n_exp = n_groups * per_group
    lg = logits[0:n_groups, :]
    le = logits[n_groups:n_groups + n_exp, :]
    g_iota = lax.broadcasted_iota(I32, lg.shape, 0)
    g_max = jnp.max(lg, axis=0, keepdims=True)
    g_sel = jnp.min(jnp.where(lg == g_max, g_iota, n_groups), axis=0, keepdims=True)
    p_sel = 1.0 / jnp.sum(jnp.exp(lg - g_max), axis=0, keepdims=True)

    e_iota = lax.broadcasted_iota(I32, le.shape, 0)
    lo = g_sel * per_group
    cand = jnp.where((e_iota >= lo) & (e_iota < lo + per_group), le, -jnp.inf)
    v1 = jnp.max(cand, axis=0, keepdims=True)
    i1 = jnp.min(jnp.where(cand == v1, e_iota, n_exp), axis=0, keepdims=True)
    cand2 = jnp.where(e_iota == i1, -jnp.inf, cand)
    v2 = jnp.max(cand2, axis=0, keepdims=True)
    i2 = jnp.min(jnp.where(cand2 == v2, e_iota, n_exp), axis=0, keepdims=True)
    e2 = jnp.exp(v2 - v1)
    inv = p_sel / (1.0 + e2)
    eidx_ref[...] = jnp.concatenate([i1, i2], axis=0)
    gate_ref[...] = jnp.concatenate([inv, e2 * inv], axis=0)


def _route(x2, nw, w_group, b_group, w_expert, b_expert, tm=512):
    t, d = x2.shape
    n_groups = w_group.shape[1]
    n_exp = w_expert.shape[1]
    n_rows = -(-(n_groups + n_exp) // LANES) * LANES
    wt = jnp.zeros((n_rows, d), F32).at[:n_groups].set(w_group.T.astype(F32))
    wt = wt.at[n_groups:n_groups + n_exp].set(w_expert.T.astype(F32))
    bias = jnp.zeros((n_rows, 1), F32).at[:n_groups, 0].set(b_group.astype(F32))
    bias = bias.at[n_groups:n_groups + n_exp, 0].set(b_expert.astype(F32))
    return pl.pallas_call(
        functools.partial(_route_kernel, n_groups=n_groups, per_group=n_exp // n_groups),
        grid=(t // tm,),
        in_specs=[pl.BlockSpec((tm, d), lambda i: (i, 0)),
                  pl.BlockSpec((1, d), lambda i: (0, 0)),
                  pl.BlockSpec((n_rows, d), lambda i: (0, 0)),
                  pl.BlockSpec((n_rows, d), lambda i: (0, 0)),
                  pl.BlockSpec((n_rows, 1), lambda i: (0, 0))],
        out_specs=[pl.BlockSpec((tm, d // 2), lambda i: (i, 0)),
                   pl.BlockSpec((MOE_TOPK, tm), lambda i: (0, i)),
                   pl.BlockSpec((MOE_TOPK, tm), lambda i: (0, i))],
        out_shape=[jax.ShapeDtypeStruct((t, d // 2), I32),
                   jax.ShapeDtypeStruct((MOE_TOPK, t), I32),
                   jax.ShapeDtypeStruct((MOE_TOPK, t), F32)],
        compiler_params=_params(("parallel",), 40),
        name="moe_route",
    )(x2, nw.reshape(1, d), *_split_bf16(wt), bias)


def _meta_kernel(eidx_ref, dest_ref, first_ref, nblk_ref, *, n_exp, n_tiles):
    tl = eidx_ref.shape[1]
    e_iota = lax.broadcasted_iota(I32, (n_exp, tl), 0)
    upper = (lax.broadcasted_iota(I32, (tl, tl), 0) < lax.broadcasted_iota(I32, (tl, tl), 1)).astype(BF16)

    def hits(i):
        return [e_iota == eidx_ref[pl.ds(k * n_tiles + i, 1), :] for k in range(MOE_TOPK)]

    def rank_tile(i, carry):
        hit = hits(i)
        member = sum(h.astype(F32) for h in hit)
        before = jnp.dot(member.astype(BF16), upper, preferred_element_type=F32) + carry
        for k in range(MOE_TOPK):
            rank = jnp.sum(jnp.where(hit[k], before, 0.0), axis=0, keepdims=True)
            dest_ref[pl.ds(k * n_tiles + i, 1), :] = rank.astype(I32)
        return carry + jnp.sum(member, axis=1, keepdims=True)

    counts = lax.fori_loop(0, n_tiles, rank_tile, jnp.zeros((n_exp, 1), F32)).astype(I32)
    nblk = (counts + (MOE_ROWS - 1)) >> MOE_ROWS_LOG2
    lower = (lax.broadcasted_iota(I32, (n_exp, n_exp), 1) < lax.broadcasted_iota(I32, (n_exp, n_exp), 0))
    first = jnp.dot(lower.astype(F32), jnp.broadcast_to(nblk.astype(F32), (n_exp, LANES)),
                    preferred_element_type=F32, precision=HIGHEST)[:, 0:1].astype(I32)
    pad_start = first << MOE_ROWS_LOG2

    def dest_tile(i, carry):
        hit = hits(i)
        for k in range(MOE_TOPK):
            base = jnp.sum(jnp.where(hit[k], pad_start, 0), axis=0, keepdims=True)
            row = pl.ds(k * n_tiles + i, 1)
            dest_ref[row, :] = dest_ref[row, :] + base
        return carry

    lax.fori_loop(0, n_tiles, dest_tile, 0)
    first_ref[...] = first
    nblk_ref[...] = nblk


def _meta(eidx, n_exp, tl=512):
    n_tok = eidx.shape[1]
    n_tiles = n_tok // tl
    dest, first, nblk = pl.pallas_call(
        functools.partial(_meta_kernel, n_exp=n_exp, n_tiles=n_tiles),
        out_shape=[jax.ShapeDtypeStruct((MOE_TOPK * n_tiles, tl), I32),
                   jax.ShapeDtypeStruct((n_exp, 1), I32),
                   jax.ShapeDtypeStruct((n_exp, 1), I32)],
        compiler_params=pltpu.CompilerParams(vmem_limit_bytes=32 * 1024 * 1024),
        name="moe_meta",
    )(eidx.reshape(MOE_TOPK * n_tiles, tl))
    return dest.reshape(MOE_TOPK, n_tok), first[:, 0], nblk[:, 0]


def _row_copy(src_ref, src_row, dst_ref, dst_row, sem):
    return pltpu.make_async_copy(src_ref.at[pl.ds(src_row, 1), :], dst_ref.at[pl.ds(dst_row, 1), :], sem)


def _dispatch_kernel(d0_ref, d1_ref, first_ref, nblk_ref, h_ref, xs_ref, zero_scr, sem, *, tt, n_exp):
    base = pl.program_id(0) * tt
    dests = (d0_ref, d1_ref)

    def pad_copy(e):
        last = pl.multiple_of((first_ref[e] + nblk_ref[e] - 1) * MOE_ROWS, MOE_ROWS)
        return pltpu.make_async_copy(zero_scr, xs_ref.at[pl.ds(last, MOE_ROWS), :], sem)

    def tail_copy(b):
        return pltpu.make_async_copy(
            zero_scr, xs_ref.at[pl.ds(pl.multiple_of(b * MOE_ROWS, MOE_ROWS), MOE_ROWS), :], sem)

    @pl.when(pl.program_id(0) == 0)
    def _():
        zero_scr[...] = jnp.zeros_like(zero_scr)
        for e in range(n_exp):
            @pl.when(nblk_ref[e] > 0)
            def _():
                pad_copy(e).start()
        for e in range(n_exp):
            @pl.when(nblk_ref[e] > 0)
            def _():
                pad_copy(e).wait()
        n_used = first_ref[n_exp - 1] + nblk_ref[n_exp - 1]
        n_blocks = xs_ref.shape[0] // MOE_ROWS
        lax.fori_loop(n_used, n_blocks, lambda b, c: (tail_copy(b).start(), c)[1], 0)
        lax.fori_loop(n_used, n_blocks, lambda b, c: (tail_copy(b).wait(), c)[1], 0)

    def start(t, c):
        for k, dref in enumerate(dests):
            _row_copy(h_ref, t, xs_ref, dref[base + t], sem).start(priority=k % DMA_THREADS)
        return c

    def wait(t, c):
        for dref in dests:
            _row_copy(h_ref, t, xs_ref, dref[base + t], sem).wait()
        return c

    lax.fori_loop(0, tt, start, 0, unroll=DMA_UNROLL)
    lax.fori_loop(0, tt, wait, 0, unroll=DMA_UNROLL)


def _dispatch(h, dest, first, nblk, n_slots, tt=2048):
    n_tok, d = h.shape
    tt = min(tt, n_tok)
    return pl.pallas_call(
        functools.partial(_dispatch_kernel, tt=tt, n_exp=first.shape[0]),
        grid_spec=pltpu.PrefetchScalarGridSpec(
            num_scalar_prefetch=4,
            grid=(n_tok // tt,),
            in_specs=[pl.BlockSpec((tt, d), lambda i, *_: (i, 0))],
            out_specs=pl.BlockSpec(memory_space=pl.ANY),
            scratch_shapes=[pltpu.VMEM((MOE_ROWS, d), h.dtype), pltpu.SemaphoreType.DMA(())]),
        out_shape=jax.ShapeDtypeStruct((n_slots, d), h.dtype),
        compiler_params=_params(("arbitrary",), 40),
        name="moe_dispatch",
    )(dest[0], dest[1], first, nblk, h)


def _expert_kernel(first_ref, nblk_ref, xs_ref, wgu_hbm, wdn_hbm, ys_ref, wgu_stage, wdn_stage, wgu_scr, wdn_scr,
                   xbuf, ybuf, sem_w, sem_x, sem_y, *, layer, hidden):
    e = pl.program_id(0)
    n_exp = pl.num_programs(0)
    nb = nblk_ref[e]
    first = first_ref[e]

    chunks = []
    for hbm, stage, scr in ((wgu_hbm, wgu_stage, wgu_scr), (wdn_hbm, wdn_stage, wdn_scr)):
        rows = stage.shape[0] // W_DMA_CHUNKS
        for c in range(W_DMA_CHUNKS):
            chunks.append((hbm, stage, scr, pl.ds(c * rows, rows), len(chunks)))

    def w_copy(ex, chunk):
        hbm, stage, _, sl, idx = chunk
        return pltpu.make_async_copy(hbm.at[layer, ex, sl, :], stage.at[sl, :], sem_w.at[idx])

    def x_copy(b, slot):
        rows = pl.ds(pl.multiple_of((first + b) * MOE_ROWS, MOE_ROWS), MOE_ROWS)
        return pltpu.make_async_copy(xs_ref.at[rows, :], xbuf.at[slot], sem_x.at[slot])

    def y_copy(b, slot):
        rows = pl.ds(pl.multiple_of((first + b) * MOE_ROWS, MOE_ROWS), MOE_ROWS)
        return pltpu.make_async_copy(ybuf.at[slot], ys_ref.at[rows, :], sem_y.at[slot])

    @pl.when(e == 0)
    def _():
        for chunk in chunks:
            w_copy(0, chunk).start(priority=W_DMA_THREAD)

    @pl.when(nb > 0)
    def _():
        x_copy(0, 0).start()

    for chunk in chunks:
        _, stage, scr, sl, _ = chunk
        w_copy(e, chunk).wait()
        scr[sl, :] = stage[sl, :].astype(BF16)

        @pl.when(e + 1 < n_exp)
        def _():
            w_copy(e + 1, chunk).start(priority=W_DMA_THREAD)

    def block(b, carry):
        slot = b & 1
        x_copy(b, slot).wait()

        @pl.when(b + 1 < nb)
        def _():
            x_copy(b + 1, 1 - slot).start()

        @pl.when(b >= 2)
        def _():
            y_copy(b - 2, slot).wait()

        gu = jnp.dot(_unpack_bf16(xbuf[slot]).astype(BF16), wgu_scr[...], preferred_element_type=F32)
        g = gu[:, :hidden]
        act = (g * _sigmoid(g)) * gu[:, hidden:]
        ybuf[slot] = _pack_bf16(jnp.dot(act.astype(BF16), wdn_scr[...], preferred_element_type=F32))
        y_copy(b, slot).start()
        return carry

    lax.fori_loop(0, nb, block, 0)

    @pl.when(nb >= 2)
    def _():
        y_copy(nb - 2, nb & 1).wait()

    @pl.when(nb >= 1)
    def _():
        y_copy(nb - 1, (nb - 1) & 1).wait()

    @pl.when(e == n_exp - 1)
    def _():
        ybuf[0] = jnp.zeros(ybuf.shape[1:], ybuf.dtype)
        n_blocks = ys_ref.shape[0] // MOE_ROWS
        lax.fori_loop(nb, n_blocks - first, lambda b, c: (y_copy(b, 0).start(), c)[1], 0)
        lax.fori_loop(nb, n_blocks - first, lambda b, c: (y_copy(b, 0).wait(), c)[1], 0)


def _experts(xs, first, nblk, layer, w_gu, w_down):
    n_exp, d, two_f = w_gu.shape[1], w_gu.shape[2], w_gu.shape[3]
    hidden = two_f // 2
    return pl.pallas_call(
        functools.partial(_expert_kernel, layer=layer, hidden=hidden),
        grid_spec=pltpu.PrefetchScalarGridSpec(
            num_scalar_prefetch=2,
            grid=(n_exp,),
            in_specs=[pl.BlockSpec(memory_space=pl.ANY),
                      pl.BlockSpec(memory_space=pl.ANY),
                      pl.BlockSpec(memory_space=pl.ANY)],
            out_specs=pl.BlockSpec(memory_space=pl.ANY),
            scratch_shapes=[pltpu.VMEM((d, two_f), w_gu.dtype), pltpu.VMEM((hidden, d), w_down.dtype),
                            pltpu.VMEM((d, two_f), BF16), pltpu.VMEM((hidden, d), BF16),
                            pltpu.VMEM((2, MOE_ROWS) + xs.shape[1:], xs.dtype),
                            pltpu.VMEM((2, MOE_ROWS) + xs.shape[1:], xs.dtype),
                            pltpu.SemaphoreType.DMA((2 * W_DMA_CHUNKS,)), pltpu.SemaphoreType.DMA((2,)),
                            pltpu.SemaphoreType.DMA((2,))]),
        out_shape=jax.ShapeDtypeStruct(xs.shape, xs.dtype),
        compiler_params=_params(("arbitrary",), 52),
        name="moe_experts",
    )(first, nblk, xs, w_gu, w_down)


def _combine_kernel(d0_ref, d1_ref, x_ref, g_ref, nw_ref, ys_ref, o_ref, buf, sem, *, tt, final_norm):
    i = pl.program_id(0)
    dests = (d0_ref, d1_ref)

    def copies(tile, slot, t):
        return [_row_copy(ys_ref, dref[tile * tt + t], buf.at[slot, k], t, sem.at[slot])
                for k, dref in enumerate(dests)]

    def start(tile, slot):
        def body(t, c):
            for k, cp in enumerate(copies(tile, slot, t)):
                cp.start(priority=k % DMA_THREADS)
            return c
        lax.fori_loop(0, tt, body, 0, unroll=DMA_UNROLL)

    def wait(tile, slot):
        def body(t, c):
            for cp in copies(tile, slot, t):
                cp.wait()
            return c
        lax.fori_loop(0, tt, body, 0, unroll=DMA_UNROLL)

    slot = i & 1

    @pl.when(i == 0)
    def _():
        start(0, 0)

    @pl.when(i + 1 < pl.num_programs(0))
    def _():
        start(i + 1, 1 - slot)

    wait(i, slot)
    g = g_ref[...]
    out = x_ref[...] + sum(g[:, k:k + 1] * _unpack_bf16(buf[slot, k]) for k in range(MOE_TOPK))
    o_ref[...] = _rms(out, nw_ref[...]) if final_norm else out


def _combine(x2, ys, dest, gates_t, final_w, tt=512):
    t, d = x2.shape
    nw = jnp.ones((1, d), F32) if final_w is None else final_w.reshape(1, d).astype(F32)
    return pl.pallas_call(
        functools.partial(_combine_kernel, tt=tt, final_norm=final_w is not None),
        grid_spec=pltpu.PrefetchScalarGridSpec(
            num_scalar_prefetch=2,
            grid=(t // tt,),
            in_specs=[pl.BlockSpec((tt, d), lambda i, d0, d1: (i, 0)),
                      pl.BlockSpec((tt, MOE_TOPK), lambda i, d0, d1: (i, 0)),
                      pl.BlockSpec((1, d), lambda i, d0, d1: (0, 0)),
                      pl.BlockSpec(memory_space=pl.ANY)],
            out_specs=pl.BlockSpec((tt, d), lambda i, d0, d1: (i, 0)),
            scratch_shapes=[pltpu.VMEM((2, MOE_TOPK, tt) + ys.shape[1:], ys.dtype),
                            pltpu.SemaphoreType.DMA((2,))]),
        out_shape=jax.ShapeDtypeStruct((t, d), F32),
        compiler_params=_params(("arbitrary",), 40),
        name="moe_combine",
    )(dest[0], dest[1], x2, gates_t, nw, ys)


def _hier_moe(x2, layer, nw, w_group, b_group, w_expert, b_expert, w_gu, w_down, final_w):
    t = x2.shape[0]
    n_exp = w_expert.shape[1]
    n_blocks = (t * MOE_TOPK) // MOE_ROWS + n_exp
    h, eidx, gates = _route(x2, nw, w_group, b_group, w_expert, b_expert)
    dest, first, nblk = _meta(eidx, n_exp)
    xs = _dispatch(h, dest, first, nblk, n_blocks * MOE_ROWS)
    ys = _experts(xs, first, nblk, layer, w_gu, w_down)
    return _combine(x2, ys, dest, gates.T, final_w)


def kernel(x, s5_lam_re, s5_lam_im, s5_log_dt, s5_b_re, s5_b_im, s5_c_re, s5_c_im, s5_d, s5_w_glu, ml_w_in, ml_b_gate, ml_g_head, ml_w_out, norm_mix, norm_ffn, moe_w_group, moe_b_group, moe_w_expert, moe_b_expert, moe_w_gu, moe_w_down, norm_final):
    bsz, seq, d = x.shape
    assert norm_mix.shape[0] == 2 and s5_w_glu.shape[0] == 1 and ml_w_in.shape[0] == 1
    x2 = x.reshape(bsz * seq, d).astype(F32)

    ops = _s5_operators(s5_lam_re[0], s5_lam_im[0], s5_log_dt[0], s5_b_re[0], s5_b_im[0],
                        s5_c_re[0], s5_c_im[0], seq // S5_CHUNK)
    g = _s5_mix(x2, _row_scale(x2), norm_mix[0].astype(F32), s5_d[0].astype(F32), ops, bsz, seq)
    x2 = _glu(x2, g, s5_w_glu[0].astype(BF16))
    x2 = _hier_moe(x2, 0, norm_ffn[0], moe_w_group[0], moe_b_group[0], moe_w_expert[0], moe_b_expert[0],
                   moe_w_gu, moe_w_down, None)

    n_heads = ml_b_gate.shape[1] // 2
    dv = ml_w_out.shape[1] // n_heads
    dk = (ml_w_in.shape[2] - 2 * n_heads * dv - 2 * n_heads) // (2 * n_heads)
    n_main = 2 * n_heads * (dk + dv)
    z, gcol = _inproj(x2, norm_mix[1], ml_w_in[0, :, :n_main].astype(BF16), ml_w_in[0, :, n_main:], ml_b_gate[0])
    a_bf = _mlstm(z, gcol, gcol.T, ml_g_head[0].astype(F32), bsz, seq, n_heads, dk, dv)
    x2 = _outproj(a_bf, ml_w_out[0].astype(BF16), x2)
    x2 = _hier_moe(x2, 1, norm_ffn[1], moe_w_group[1], moe_b_group[1], moe_w_expert[1], moe_b_expert[1],
                   moe_w_gu, moe_w_down, norm_final)
    return x2.reshape(bsz, seq, d).astype(x.dtype)
```

```python
import functools
import math

import jax
import jax.numpy as jnp
from jax import lax
from jax.experimental import pallas as pl
from jax.experimental.pallas import tpu as pltpu

F32 = jnp.float32
BF16 = jnp.bfloat16
I32 = jnp.int32

EPS = 1e-6
GATE_SOFTCAP = 15.0
MOE_TOPK = 2

LANES = 128
S5_CHUNK = 16
S5_OUT_STEPS = 2
S5_BATCH = 2
ML_CHUNK = 256
ML_HEAD_GROUP = 4
MOE_ROWS_LOG2 = 8
MOE_ROWS = 1 << MOE_ROWS_LOG2
VMEM_BYTES = 64 * 1024 * 1024
DMA_UNROLL = 16
W_DMA_CHUNKS = 4
DMA_THREADS = 2
W_DMA_THREAD = 1
HIGHEST = lax.Precision.HIGHEST


def _params(sem, vmem_mib):
    assert vmem_mib * 1024 * 1024 < VMEM_BYTES
    return pltpu.CompilerParams(dimension_semantics=sem, vmem_limit_bytes=vmem_mib * 1024 * 1024)


def _rms(x32, w):
    ms = jnp.mean(x32 * x32, axis=-1, keepdims=True)
    return x32 * lax.rsqrt(ms + EPS) * w


def _sigmoid(x):
    return 1.0 / (1.0 + jnp.exp(-x))


def _gelu_tanh(x):
    c = math.sqrt(2.0 / math.pi)
    return 0.5 * x * (1.0 + jnp.tanh(c * (x + 0.044715 * (x * x * x))))


def _split_bf16(a32):
    hi = a32.astype(BF16)
    return hi, (a32 - hi.astype(F32)).astype(BF16)


def _dot3(a_hi, a_lo, b_hi, b_lo, dot):
    return (dot(a_hi, b_hi, preferred_element_type=F32) + dot(a_lo, b_hi, preferred_element_type=F32)
            + dot(a_hi, b_lo, preferred_element_type=F32))


def _dot_nt(a, b, preferred_element_type):
    return lax.dot_general(a, b, (((1,), (1,)), ((), ())), preferred_element_type=preferred_element_type)


HI16 = -65536
LO16 = 65535


def _pack_bf16(v):
    half = v.shape[1] // 2
    bits = lax.bitcast_convert_type(v.astype(BF16).astype(F32), I32)
    return (bits[:, half:] & HI16) | ((bits[:, :half] >> 16) & LO16)


def _unpack_bf16(w):
    lo = lax.bitcast_convert_type(w << 16, F32)
    hi = lax.bitcast_convert_type(w & HI16, F32)
    return jnp.concatenate([lo, hi], axis=1)


def _row_scale_kernel(x_ref, r_ref):
    x = x_ref[...]
    r_ref[...] = lax.rsqrt(jnp.mean(x * x, axis=-1, keepdims=True) + EPS)


def _row_scale(x2, tm=512):
    t, d = x2.shape
    return pl.pallas_call(
        _row_scale_kernel,
        grid=(t // tm,),
        in_specs=[pl.BlockSpec((tm, d), lambda i: (i, 0))],
        out_specs=pl.BlockSpec((tm, 1), lambda i: (i, 0)),
        out_shape=jax.ShapeDtypeStruct((t, 1), F32),
        compiler_params=_params(("parallel",), 32),
        name="row_scale",
    )(x2)


def _s5_build_operators(bexp_ref, cexp_ref, qpow_ref, ppow_ref, win_scr, wout_scr, toep_scr):
    L = qpow_ref.shape[1]
    half = bexp_ref.shape[2] // 2

    def tile(ref, s, t):
        return ref.at[s * LANES:(s + 1) * LANES, t * LANES:(t + 1) * LANES]

    br, bi = bexp_ref[0, :, :half], bexp_ref[0, :, half:]
    cr, ci = cexp_ref[0, :half, :], cexp_ref[0, half:, :]
    b_cat = bexp_ref[0].astype(BF16)
    for s in range(L):
        qr, qi = qpow_ref[0, s:s + 1, :half], qpow_ref[0, s:s + 1, half:]
        win_scr[s * LANES:(s + 1) * LANES, :] = jnp.concatenate(
            [br * qr - bi * qi, br * qi + bi * qr], axis=1).astype(BF16)
    for t in range(L + 1):
        pr, pi = ppow_ref[0, :half, t:t + 1], ppow_ref[0, half:, t:t + 1]
        w_t = jnp.concatenate([cr * pr - ci * pi, -(cr * pi + ci * pr)], axis=0).astype(BF16)
        if t >= 1:
            wout_scr[:, (t - 1) * LANES:t * LANES] = w_t
        if t < L:
            tap = jnp.dot(b_cat, w_t, preferred_element_type=F32).astype(BF16)
            for s in range(L - t):
                tile(toep_scr, s, s + t)[...] = tap
    zero = jnp.zeros((LANES, LANES), BF16)
    for s in range(L):
        for tau in range(s):
            tile(toep_scr, s, tau)[...] = zero


def _s5_kernel(x_ref, r_ref, nw_ref, dsk_ref, bexp_ref, cexp_ref, qpow_ref, ppow_ref, lpow_ref, g_ref,
               win_scr, wout_scr, toep_scr, u_scr, *, n_chunks):
    L = qpow_ref.shape[1]
    rows = u_scr.shape[0]

    @pl.when(pl.program_id(1) == 0)
    def _():
        _s5_build_operators(bexp_ref, cexp_ref, qpow_ref, ppow_ref, win_scr, wout_scr, toep_scr)

    def h_step(s):
        return x_ref[pl.ds(s, rows, stride=L), :] * r_ref[:, s:s + 1] * nw_ref[...]

    for s in range(L):
        u_scr[:, s * LANES:(s + 1) * LANES] = h_step(s).astype(BF16)
    v = jnp.dot(u_scr[...], win_scr[...], preferred_element_type=F32)
    half = v.shape[1] // 2
    assert n_chunks & (n_chunks - 1) == 0 and rows % n_chunks == 0
    cidx = lax.broadcasted_iota(I32, (rows, 1), 0) & (n_chunks - 1)

    def shifted(a, d):
        return jnp.where(cidx >= d, pltpu.roll(a, d, 0), 0.0)

    xr = shifted(v[:, :half], 1)
    xi = shifted(v[:, half:], 1)
    d, k = 1, 0
    while d < n_chunks:
        ar = lpow_ref[0, k:k + 1, :half]
        ai = lpow_ref[0, k:k + 1, half:]
        sr = shifted(xr, d)
        si = shifted(xi, d)
        xr, xi = xr + (ar * sr - ai * si), xi + (ar * si + ai * sr)
        d, k = 2 * d, k + 1
    state = jnp.concatenate([xr, xi], axis=1).astype(BF16)
    for j in range(L // S5_OUT_STEPS):
        lo, hi = j * S5_OUT_STEPS * LANES, (j + 1) * S5_OUT_STEPS * LANES
        y = (jnp.dot(u_scr[:, :hi], toep_scr[:hi, lo:hi], preferred_element_type=F32)
             + jnp.dot(state, wout_scr[:, lo:hi], preferred_element_type=F32))
        for k in range(S5_OUT_STEPS):
            tau = j * S5_OUT_STEPS + k
            g_ref[pl.ds(tau, rows, stride=L), :] = _gelu_tanh(
                y[:, k * LANES:(k + 1) * LANES] + dsk_ref[...] * h_step(tau))


def _s5_operators(lam_re, lam_im, log_dt, b_re, b_im, c_re, c_im, n_chunks):
    g, p, cg = b_re.shape
    L = S5_CHUNK
    gt = LANES // cg
    nt = g // gt
    lam = lax.complex(lam_re.astype(F32), lam_im.astype(F32))
    dt = jnp.exp(log_dt.astype(F32))[:, None]
    z = lam * dt
    lam_bar = jnp.exp(z)
    b_bar = ((lam_bar - 1.0) / lam)[..., None] * lax.complex(b_re.astype(F32), b_im.astype(F32))
    cc = lax.complex(c_re.astype(F32), c_im.astype(F32))
    steps = jnp.arange(L + 1, dtype=F32)
    pw = jnp.exp(z[:, None, :] * steps[None, :, None])
    eye = jnp.eye(gt, dtype=F32)

    def re_im(a, axis):
        return jnp.stack([jnp.real(a), jnp.imag(a)], axis=axis)

    def per_tile(a):
        k = a.shape[1]
        return jnp.transpose(a.reshape(nt, gt, k, p), (0, 2, 1, 3)).reshape(nt, k, gt * p)

    bexp = re_im(jnp.transpose(b_bar, (0, 2, 1)), 2).reshape(nt, gt, cg, 2, p)
    bexp = jnp.einsum('iacrp,ab->iacrbp', bexp, eye).reshape(nt, gt * cg, 2 * gt * p)
    cexp = re_im(jnp.transpose(cc, (0, 2, 1)), 1).reshape(nt, gt, 2, p, cg)
    cexp = jnp.einsum('iarpc,ab->irapbc', cexp, eye).reshape(nt, 2 * gt * p, gt * cg)
    q = pw[:, L - 1 - jnp.arange(L), :]
    qpow = jnp.concatenate([per_tile(jnp.real(q)), per_tile(jnp.imag(q))], axis=-1)
    pt = jnp.concatenate([per_tile(jnp.real(pw)), per_tile(jnp.imag(pw))], axis=-1)
    ppow = jnp.zeros((nt, 2 * gt * p, LANES), F32).at[:, :, :L + 1].set(jnp.transpose(pt, (0, 2, 1)))

    n_steps = max(1, (n_chunks - 1).bit_length())
    hops = (2.0 ** jnp.arange(n_steps, dtype=F32)) * L
    lp = jnp.exp(z[:, None, :] * hops[None, :, None])
    lpow = jnp.concatenate([per_tile(jnp.real(lp)), per_tile(jnp.imag(lp))], axis=-1)
    return bexp, cexp, qpow, ppow, lpow


def _s5_mix(x2, r, nw, d_skip, ops, bsz, seq):
    t, d = x2.shape
    L = S5_CHUNK
    n_chunks = seq // L
    nt = d // LANES
    n_state = ops[0].shape[2]
    r2 = r.reshape(t // L, L)
    bb = S5_BATCH if bsz % S5_BATCH == 0 else 1
    rows = bb * n_chunks
    return pl.pallas_call(
        functools.partial(_s5_kernel, n_chunks=n_chunks),
        grid=(nt, bsz // bb),
        in_specs=[pl.BlockSpec((bb * seq, LANES), lambda i, b: (b, i)),
                  pl.BlockSpec((rows, L), lambda i, b: (b, 0)),
                  pl.BlockSpec((1, LANES), lambda i, b: (0, i)),
                  pl.BlockSpec((1, LANES), lambda i, b: (0, i))]
                 + [pl.BlockSpec((1,) + a.shape[1:], lambda i, b: (i, 0, 0)) for a in ops],
        out_specs=pl.BlockSpec((bb * seq, LANES), lambda i, b: (b, i)),
        out_shape=jax.ShapeDtypeStruct((t, d), F32),
        scratch_shapes=[pltpu.VMEM((L * LANES, n_state), BF16), pltpu.VMEM((n_state, L * LANES), BF16),
                        pltpu.VMEM((L * LANES, L * LANES), BF16), pltpu.VMEM((rows, L * LANES), BF16)],
        compiler_params=_params(("arbitrary", "arbitrary"), 56),
        name="s5_scan",
    )(x2, r2, nw.reshape(1, d), d_skip.reshape(1, d), *ops)


def _glu_kernel(g_ref, wv_ref, wg_ref, xres_ref, o_ref, g_scr):
    @pl.when(pl.program_id(1) == 0)
    def _():
        g_scr[...] = g_ref[...].astype(BF16)

    g = g_scr[...]
    val = jnp.dot(g, wv_ref[...], preferred_element_type=F32)
    gate = jnp.dot(g, wg_ref[...], preferred_element_type=F32)
    o_ref[...] = xres_ref[...] + val * _sigmoid(gate)


def _glu(x2, g, w_glu_bf, tm=1024, tn=512):
    t, d = x2.shape
    nj = d // tn
    return pl.pallas_call(
        _glu_kernel,
        grid=(t // tm, nj),
        in_specs=[pl.BlockSpec((tm, d), lambda i, j: (i, 0)),
                  pl.BlockSpec((d, tn), lambda i, j: (0, j)),
                  pl.BlockSpec((d, tn), lambda i, j: (0, j + nj)),
                  pl.BlockSpec((tm, tn), lambda i, j: (i, j))],
        out_specs=pl.BlockSpec((tm, tn), lambda i, j: (i, j)),
        out_shape=jax.ShapeDtypeStruct((t, d), F32),
        scratch_shapes=[pltpu.VMEM((tm, d), BF16)],
        compiler_params=_params(("parallel", "arbitrary"), 48),
        name="s5_glu",
    )(g, w_glu_bf, w_glu_bf, x2)


def _inproj_kernel(x_ref, nw_ref, w_ref, wg_hi_ref, wg_lo_ref, bg_ref, z_ref, g_ref, h_scr, *, n_heads):
    @pl.when(pl.program_id(1) == 0)
    def _():
        h_hi, h_lo = _split_bf16(_rms(x_ref[...], nw_ref[...]))
        h_scr[...] = h_hi
        pre = _dot3(h_hi, h_lo, wg_hi_ref[...], wg_lo_ref[...], jnp.dot) + bg_ref[...]
        t = GATE_SOFTCAP * jnp.tanh(pre / GATE_SOFTCAP)
        log_sig = jnp.minimum(t, 0.0) - jnp.log(1.0 + jnp.exp(-jnp.abs(t)))
        g_ref[...] = jnp.where(lax.broadcasted_iota(I32, pre.shape, 1) >= n_heads, log_sig, t)

    z_ref[...] = jnp.dot(h_scr[...], w_ref[...], preferred_element_type=F32).astype(z_ref.dtype)


def _inproj(x2, nw, w_bf, w_gate, b_gate, tm=1024, tn=1024):
    t, d = x2.shape
    n = w_bf.shape[1]
    ng = w_gate.shape[1]
    wg_hi, wg_lo = _split_bf16(w_gate.astype(F32))
    return pl.pallas_call(
        functools.partial(_inproj_kernel, n_heads=ng // 2),
        grid=(t // tm, n // tn),
        in_specs=[pl.BlockSpec((tm, d), lambda i, j: (i, 0)),
                  pl.BlockSpec((1, d), lambda i, j: (0, 0)),
                  pl.BlockSpec((d, tn), lambda i, j: (0, j)),
                  pl.BlockSpec((d, ng), lambda i, j: (0, 0)),
                  pl.BlockSpec((d, ng), lambda i, j: (0, 0)),
                  pl.BlockSpec((1, ng), lambda i, j: (0, 0))],
        out_specs=[pl.BlockSpec((tm, tn), lambda i, j: (i, j)), pl.BlockSpec((tm, ng), lambda i, j: (i, 0))],
        out_shape=[jax.ShapeDtypeStruct((t, n), BF16), jax.ShapeDtypeStruct((t, ng), F32)],
        scratch_shapes=[pltpu.VMEM((tm, d), BF16)],
        compiler_params=_params(("parallel", "arbitrary"), 48),
        name="ml_inproj",
    )(x2, nw.reshape(1, d), w_bf, wg_hi, wg_lo, b_gate.reshape(1, ng).astype(F32))


def _mlstm_head(head, q, k, v, o, gc, gr, g_head, c_ref, m_ref, n_heads, scale):
    lane_h = lax.broadcasted_iota(I32, gc.shape, 1)
    sub_h = lax.broadcasted_iota(I32, gr.shape, 0)
    i_col = jnp.sum(jnp.where(lane_h == head, gc, 0.0), axis=1, keepdims=True)
    f_col = jnp.sum(jnp.where(lane_h == head + n_heads, gc, 0.0), axis=1, keepdims=True)
    i_row = jnp.sum(jnp.where(sub_h == head, gr, 0.0), axis=0, keepdims=True)
    f_row = jnp.sum(jnp.where(sub_h == head + n_heads, gr, 0.0), axis=0, keepdims=True)

    L = gc.shape[0]
    t_idx = lax.broadcasted_iota(I32, (L, L), 0)
    s_idx = lax.broadcasted_iota(I32, (L, L), 1)
    causal = s_idx <= t_idx
    bcum_col = jnp.sum(jnp.where(causal, f_row, 0.0), axis=1, keepdims=True)
    bcum_row = jnp.sum(jnp.where(t_idx <= s_idx, f_col, 0.0), axis=0, keepdims=True)

    m_prev = m_ref[0:1, 0:1]
    a_log = bcum_col + m_prev
    d_log = jnp.where(causal, bcum_col - bcum_row + i_row, -jnp.inf)
    m_t = jnp.maximum(a_log, jnp.max(d_log, axis=1, keepdims=True))
    w_intra = jnp.exp(d_log - m_t)
    w_inter = jnp.exp(a_log - m_t)

    dv = v.shape[1]
    ones_col = (lax.broadcasted_iota(I32, (L, LANES), 1) == 0).astype(F32)
    v_ext = jnp.concatenate([v, ones_col.astype(BF16)], axis=1)
    c_ext = c_ref[...]
    sc = lax.dot_general(q, k, (((1,), (1,)), ((), ())), preferred_element_type=F32) * (w_intra * scale)
    num_den = (w_inter * jnp.dot(q, c_ext.astype(BF16), preferred_element_type=F32)
               + jnp.dot(sc.astype(BF16), v_ext, preferred_element_type=F32))
    den = num_den[:, dv:dv + 1]
    h_out = num_den[:, :dv] / jnp.maximum(jnp.abs(den), jnp.exp(-m_t))

    b_last = bcum_col[L - 1:L, :]
    g_log = b_last - bcum_col + i_col
    m_new = jnp.maximum(b_last + m_prev, jnp.max(g_log, axis=0, keepdims=True))
    decay = jnp.exp(b_last + m_prev - m_new)
    wk = jnp.exp(g_log - m_new) * scale
    wv = jnp.concatenate([(wk * v.astype(F32)).astype(BF16), (wk * ones_col).astype(BF16)], axis=1)
    kv = lax.dot_general(k, wv, (((0,), (0,)), ((), ())), preferred_element_type=F32)
    c_ref[...] = decay * c_ext + kv
    m_ref[...] = jnp.broadcast_to(m_new, m_ref.shape)

    hn = h_out * lax.rsqrt(jnp.mean(h_out * h_out, axis=-1, keepdims=True) + EPS) * g_head
    return _sigmoid(o.astype(F32)) * hn


def _mlstm_kernel(q_ref, k_ref, v_ref, o_ref, gc_ref, gr_ref, gh_ref, out_ref, c_scr, m_scr,
                  *, n_heads, scale):
    @pl.when(pl.program_id(2) == 0)
    def _():
        c_scr[...] = jnp.zeros_like(c_scr)
        m_scr[...] = jnp.zeros_like(m_scr)

    group, dk = c_scr.shape[:2]
    dv = c_scr.shape[2] - LANES
    gc = gc_ref[...]
    gr = gr_ref[...]
    for j in range(group):
        qk_cols = slice(j * dk, (j + 1) * dk)
        v_cols = slice(j * dv, (j + 1) * dv)
        out = _mlstm_head(pl.program_id(1) * group + j, q_ref[:, qk_cols], k_ref[:, qk_cols], v_ref[:, v_cols],
                          o_ref[:, v_cols], gc, gr, gh_ref[:, v_cols], c_scr.at[j], m_scr.at[j], n_heads, scale)
        out_ref[:, v_cols] = out.astype(out_ref.dtype)


def _mlstm(z, gcol, grow, g_head, bsz, seq, n_heads, dk, dv):
    t = bsz * seq
    L = ML_CHUNK
    nc = seq // L
    hp = ML_HEAD_GROUP if n_heads % ML_HEAD_GROUP == 0 else 1
    qk_w = n_heads * dk
    ng = 2 * n_heads
    k_off = qk_w // (hp * dk)
    v_off = (2 * qk_w) // (hp * dv)
    o_off = (2 * qk_w + n_heads * dv) // (hp * dv)
    return pl.pallas_call(
        functools.partial(_mlstm_kernel, n_heads=n_heads, scale=dk ** -0.5),
        grid=(bsz, n_heads // hp, nc),
        in_specs=[pl.BlockSpec((L, hp * dk), lambda b, h, c: (b * nc + c, h)),
                  pl.BlockSpec((L, hp * dk), lambda b, h, c: (b * nc + c, k_off + h)),
                  pl.BlockSpec((L, hp * dv), lambda b, h, c: (b * nc + c, v_off + h)),
                  pl.BlockSpec((L, hp * dv), lambda b, h, c: (b * nc + c, o_off + h)),
                  pl.BlockSpec((L, ng), lambda b, h, c: (b * nc + c, 0)),
                  pl.BlockSpec((ng, L), lambda b, h, c: (0, b * nc + c)),
                  pl.BlockSpec((1, hp * dv), lambda b, h, c: (0, h))],
        out_specs=pl.BlockSpec((L, hp * dv), lambda b, h, c: (b * nc + c, h)),
        out_shape=jax.ShapeDtypeStruct((t, n_heads * dv), BF16),
        scratch_shapes=[pltpu.VMEM((hp, dk, dv + LANES), F32), pltpu.VMEM((hp, 1, LANES), F32)],
        compiler_params=_params(("parallel", "parallel", "arbitrary"), 40),
        name="ml_chunk",
    )(z, z, z, z, gcol, grow, g_head.reshape(1, n_heads * dv))


def _outproj_kernel(a_ref, w_ref, xres_ref, o_ref):
    o_ref[...] = xres_ref[...] + jnp.dot(a_ref[...], w_ref[...], preferred_element_type=F32)


def _outproj(a_bf, w_bf, x2, tm=1024, tn=1024):
    t, kdim = a_bf.shape
    d = w_bf.shape[1]
    return pl.pallas_call(
        _outproj_kernel,
        grid=(t // tm, d // tn),
        in_specs=[pl.BlockSpec((tm, kdim), lambda i, j: (i, 0)),
                  pl.BlockSpec((kdim, tn), lambda i, j: (0, j)),
                  pl.BlockSpec((tm, tn), lambda i, j: (i, j))],
        out_specs=pl.BlockSpec((tm, tn), lambda i, j: (i, j)),
        out_shape=jax.ShapeDtypeStruct((t, d), F32),
        compiler_params=_params(("parallel", "arbitrary"), 40),
        name="ml_outproj",
    )(a_bf, w_bf, x2)


def _route_kernel(x_ref, nw_ref, wt_hi_ref, wt_lo_ref, b_ref, h_ref, eidx_ref, gate_ref, *, n_groups, per_group):
    h_hi, h_lo = _split_bf16(_rms(x_ref[...], nw_ref[...]))
    h_ref[...] = _pack_bf16(h_hi)
    logits = _dot3(wt_hi_ref[...], wt_lo_ref[...], h_hi, h_lo, _dot_nt) + b_ref[...]
    n_exp = n_groups * per_group
    lg = logits[0:n_groups, :]
    le = logits[n_groups:n_groups + n_exp, :]
    g_iota = lax.broadcasted_iota(I32, lg.shape, 0)
    g_max = jnp.max(lg, axis=0, keepdims=True)
    g_sel = jnp.min(jnp.where(lg == g_max, g_iota, n_groups), axis=0, keepdims=True)
    p_sel = 1.0 / jnp.sum(jnp.exp(lg - g_max), axis=0, keepdims=True)

    e_iota = lax.broadcasted_iota(I32, le.shape, 0)
    lo = g_sel * per_group
    cand = jnp.where((e_iota >= lo) & (e_iota < lo + per_group), le, -jnp.inf)
    v1 = jnp.max(cand, axis=0, keepdims=True)
    i1 = jnp.min(jnp.where(cand == v1, e_iota, n_exp), axis=0, keepdims=True)
    cand2 = jnp.where(e_iota == i1, -jnp.inf, cand)
    v2 = jnp.max(cand2, axis=0, keepdims=True)
    i2 = jnp.min(jnp.where(cand2 == v2, e_iota, n_exp), axis=0, keepdims=True)
    e2 = jnp.exp(v2 - v1)
    inv = p_sel / (1.0 + e2)
    eidx_ref[...] = jnp.concatenate([i1, i2], axis=0)
    gate_ref[...] = jnp.concatenate([inv, e2 * inv], axis=0)


def _route(x2, nw, w_group, b_group, w_expert, b_expert, tm=1024):
    t, d = x2.shape
    n_groups = w_group.shape[1]
    n_exp = w_expert.shape[1]
    n_rows = -(-(n_groups + n_exp) // LANES) * LANES
    wt = jnp.zeros((n_rows, d), F32).at[:n_groups].set(w_group.T.astype(F32))
    wt = wt.at[n_groups:n_groups + n_exp].set(w_expert.T.astype(F32))
    bias = jnp.zeros((n_rows, 1), F32).at[:n_groups, 0].set(b_group.astype(F32))
    bias = bias.at[n_groups:n_groups + n_exp, 0].set(b_expert.astype(F32))
    return pl.pallas_call(
        functools.partial(_route_kernel, n_groups=n_groups, per_group=n_exp // n_groups),
        grid=(t // tm,),
        in_specs=[pl.BlockSpec((tm, d), lambda i: (i, 0)),
                  pl.BlockSpec((1, d), lambda i: (0, 0)),
                  pl.BlockSpec((n_rows, d), lambda i: (0, 0)),
                  pl.BlockSpec((n_rows, d), lambda i: (0, 0)),
                  pl.BlockSpec((n_rows, 1), lambda i: (0, 0))],
        out_specs=[pl.BlockSpec((tm, d // 2), lambda i: (i, 0)),
                   pl.BlockSpec((MOE_TOPK, tm), lambda i: (0, i)),
                   pl.BlockSpec((MOE_TOPK, tm), lambda i: (0, i))],
        out_shape=[jax.ShapeDtypeStruct((t, d // 2), I32),
                   jax.ShapeDtypeStruct((MOE_TOPK, t), I32),
                   jax.ShapeDtypeStruct((MOE_TOPK, t), F32)],
        compiler_params=_params(("parallel",), 40),
        name="moe_route",
    )(x2, nw.reshape(1, d), *_split_bf16(wt), bias)


def _meta_kernel(eidx_ref, dest_ref, first_ref, nblk_ref, *, n_exp, n_tiles):
    tl = eidx_ref.shape[1]
    e_iota = lax.broadcasted_iota(I32, (n_exp, tl), 0)
    upper = (lax.broadcasted_iota(I32, (tl, tl), 0) < lax.broadcasted_iota(I32, (tl, tl), 1)).astype(BF16)

    def hits(i):
        return [e_iota == eidx_ref[pl.ds(k * n_tiles + i, 1), :] for k in range(MOE_TOPK)]

    def rank_tile(i, carry):
        hit = hits(i)
        member = sum(h.astype(F32) for h in hit)
        before = jnp.dot(member.astype(BF16), upper, preferred_element_type=F32) + carry
        for k in range(MOE_TOPK):
            rank = jnp.sum(jnp.where(hit[k], before, 0.0), axis=0, keepdims=True)
            dest_ref[pl.ds(k * n_tiles + i, 1), :] = rank.astype(I32)
        return carry + jnp.sum(member, axis=1, keepdims=True)

    counts = lax.fori_loop(0, n_tiles, rank_tile, jnp.zeros((n_exp, 1), F32)).astype(I32)
    nblk = (counts + (MOE_ROWS - 1)) >> MOE_ROWS_LOG2
    lower = (lax.broadcasted_iota(I32, (n_exp, n_exp), 1) < lax.broadcasted_iota(I32, (n_exp, n_exp), 0))
    first = jnp.dot(lower.astype(F32), jnp.broadcast_to(nblk.astype(F32), (n_exp, LANES)),
                    preferred_element_type=F32, precision=HIGHEST)[:, 0:1].astype(I32)
    pad_start = first << MOE_ROWS_LOG2

    def dest_tile(i, carry):
        hit = hits(i)
        for k in range(MOE_TOPK):
            base = jnp.sum(jnp.where(hit[k], pad_start, 0), axis=0, keepdims=True)
            row = pl.ds(k * n_tiles + i, 1)
            dest_ref[row, :] = dest_ref[row, :] + base
        return carry

    lax.fori_loop(0, n_tiles, dest_tile, 0)
    first_ref[...] = first
    nblk_ref[...] = nblk


def _meta(eidx, n_exp, tl=512):
    n_tok = eidx.shape[1]
    n_tiles = n_tok // tl
    dest, first, nblk = pl.pallas_call(
        functools.partial(_meta_kernel, n_exp=n_exp, n_tiles=n_tiles),
        out_shape=[jax.ShapeDtypeStruct((MOE_TOPK * n_tiles, tl), I32),
                   jax.ShapeDtypeStruct((n_exp, 1), I32),
                   jax.ShapeDtypeStruct((n_exp, 1), I32)],
        compiler_params=pltpu.CompilerParams(vmem_limit_bytes=32 * 1024 * 1024),
        name="moe_meta",
    )(eidx.reshape(MOE_TOPK * n_tiles, tl))
    return dest.reshape(MOE_TOPK, n_tok), first[:, 0], nblk[:, 0]


def _row_copy(src_ref, src_row, dst_ref, dst_row, sem):
    return pltpu.make_async_copy(src_ref.at[pl.ds(src_row, 1), :], dst_ref.at[pl.ds(dst_row, 1), :], sem)


def _dispatch_kernel(d0_ref, d1_ref, first_ref, nblk_ref, h_ref, xs_ref, zero_scr, sem, *, tt, n_exp):
    base = pl.program_id(0) * tt
    dests = (d0_ref, d1_ref)

    def pad_copy(e):
        last = pl.multiple_of((first_ref[e] + nblk_ref[e] - 1) * MOE_ROWS, MOE_ROWS)
        return pltpu.make_async_copy(zero_scr, xs_ref.at[pl.ds(last, MOE_ROWS), :], sem)

    def tail_copy(b):
        return pltpu.make_async_copy(
            zero_scr, xs_ref.at[pl.ds(pl.multiple_of(b * MOE_ROWS, MOE_ROWS), MOE_ROWS), :], sem)

    @pl.when(pl.program_id(0) == 0)
    def _():
        zero_scr[...] = jnp.zeros_like(zero_scr)
        for e in range(n_exp):
            @pl.when(nblk_ref[e] > 0)
            def _():
                pad_copy(e).start()
        for e in range(n_exp):
            @pl.when(nblk_ref[e] > 0)
            def _():
                pad_copy(e).wait()
        n_used = first_ref[n_exp - 1] + nblk_ref[n_exp - 1]
        n_blocks = xs_ref.shape[0] // MOE_ROWS
        lax.fori_loop(n_used, n_blocks, lambda b, c: (tail_copy(b).start(), c)[1], 0)
        lax.fori_loop(n_used, n_blocks, lambda b, c: (tail_copy(b).wait(), c)[1], 0)

    def start(t, c):
        for k, dref in enumerate(dests):
            _row_copy(h_ref, t, xs_ref, dref[base + t], sem).start(priority=k % DMA_THREADS)
        return c

    def wait(t, c):
        for dref in dests:
            _row_copy(h_ref, t, xs_ref, dref[base + t], sem).wait()
        return c

    lax.fori_loop(0, tt, start, 0, unroll=DMA_UNROLL)
    lax.fori_loop(0, tt, wait, 0, unroll=DMA_UNROLL)


def _dispatch(h, dest, first, nblk, n_slots, tt=2048):
    n_tok, d = h.shape
    tt = min(tt, n_tok)
    return pl.pallas_call(
        functools.partial(_dispatch_kernel, tt=tt, n_exp=first.shape[0]),
        grid_spec=pltpu.PrefetchScalarGridSpec(
            num_scalar_prefetch=4,
            grid=(n_tok // tt,),
            in_specs=[pl.BlockSpec((tt, d), lambda i, *_: (i, 0))],
            out_specs=pl.BlockSpec(memory_space=pl.ANY),
            scratch_shapes=[pltpu.VMEM((MOE_ROWS, d), h.dtype), pltpu.SemaphoreType.DMA(())]),
        out_shape=jax.ShapeDtypeStruct((n_slots, d), h.dtype),
        compiler_params=_params(("arbitrary",), 40),
        name="moe_dispatch",
    )(dest[0], dest[1], first, nblk, h)


def _expert_kernel(first_ref, nblk_ref, xs_ref, wgu_hbm, wdn_hbm, ys_ref, wgu_stage, wdn_stage, wgu_scr, wdn_scr,
                   xbuf, ybuf, sem_w, sem_x, sem_y, *, layer, hidden):
    e = pl.program_id(0)
    n_exp = pl.num_programs(0)
    nb = nblk_ref[e]
    first = first_ref[e]

    chunks = []
    for hbm, stage, scr in ((wgu_hbm, wgu_stage, wgu_scr), (wdn_hbm, wdn_stage, wdn_scr)):
        rows = stage.shape[0] // W_DMA_CHUNKS
        for c in range(W_DMA_CHUNKS):
            chunks.append((hbm, stage, scr, pl.ds(c * rows, rows), len(chunks)))

    def w_copy(ex, chunk):
        hbm, stage, _, sl, idx = chunk
        return pltpu.make_async_copy(hbm.at[layer, ex, sl, :], stage.at[sl, :], sem_w.at[idx])

    def x_copy(b, slot):
        rows = pl.ds(pl.multiple_of((first + b) * MOE_ROWS, MOE_ROWS), MOE_ROWS)
        return pltpu.make_async_copy(xs_ref.at[rows, :], xbuf.at[slot], sem_x.at[slot])

    def y_copy(b, slot):
        rows = pl.ds(pl.multiple_of((first + b) * MOE_ROWS, MOE_ROWS), MOE_ROWS)
        return pltpu.make_async_copy(ybuf.at[slot], ys_ref.at[rows, :], sem_y.at[slot])

    @pl.when(e == 0)
    def _():
        for chunk in chunks:
            w_copy(0, chunk).start(priority=W_DMA_THREAD)

    @pl.when(nb > 0)
    def _():
        x_copy(0, 0).start()

    for chunk in chunks:
        _, stage, scr, sl, _ = chunk
        w_copy(e, chunk).wait()
        scr[sl, :] = stage[sl, :].astype(BF16)

        @pl.when(e + 1 < n_exp)
        def _():
            w_copy(e + 1, chunk).start(priority=W_DMA_THREAD)

    def block(b, carry):
        slot = b & 1
        x_copy(b, slot).wait()

        @pl.when(b + 1 < nb)
        def _():
            x_copy(b + 1, 1 - slot).start()

        @pl.when(b >= 2)
        def _():
            y_copy(b - 2, slot).wait()

        gu = jnp.dot(_unpack_bf16(xbuf[slot]).astype(BF16), wgu_scr[...], preferred_element_type=F32)
        g = gu[:, :hidden]
        act = (g * _sigmoid(g)) * gu[:, hidden:]
        ybuf[slot] = _pack_bf16(jnp.dot(act.astype(BF16), wdn_scr[...], preferred_element_type=F32))
        y_copy(b, slot).start()
        return carry

    lax.fori_loop(0, nb, block, 0)

    @pl.when(nb >= 2)
    def _():
        y_copy(nb - 2, nb & 1).wait()

    @pl.when(nb >= 1)
    def _():
        y_copy(nb - 1, (nb - 1) & 1).wait()

    @pl.when(e == n_exp - 1)
    def _():
        ybuf[0] = jnp.zeros(ybuf.shape[1:], ybuf.dtype)
        n_blocks = ys_ref.shape[0] // MOE_ROWS
        lax.fori_loop(nb, n_blocks - first, lambda b, c: (y_copy(b, 0).start(), c)[1], 0)
        lax.fori_loop(nb, n_blocks - first, lambda b, c: (y_copy(b, 0).wait(), c)[1], 0)


def _experts(xs, first, nblk, layer, w_gu, w_down):
    n_exp, d, two_f = w_gu.shape[1], w_gu.shape[2], w_gu.shape[3]
    hidden = two_f // 2
    return pl.pallas_call(
        functools.partial(_expert_kernel, layer=layer, hidden=hidden),
        grid_spec=pltpu.PrefetchScalarGridSpec(
            num_scalar_prefetch=2,
            grid=(n_exp,),
            in_specs=[pl.BlockSpec(memory_space=pl.ANY),
                      pl.BlockSpec(memory_space=pl.ANY),
                      pl.BlockSpec(memory_space=pl.ANY)],
            out_specs=pl.BlockSpec(memory_space=pl.ANY),
            scratch_shapes=[pltpu.VMEM((d, two_f), w_gu.dtype), pltpu.VMEM((hidden, d), w_down.dtype),
                            pltpu.VMEM((d, two_f), BF16), pltpu.VMEM((hidden, d), BF16),
                            pltpu.VMEM((2, MOE_ROWS) + xs.shape[1:], xs.dtype),
                            pltpu.VMEM((2, MOE_ROWS) + xs.shape[1:], xs.dtype),
                            pltpu.SemaphoreType.DMA((2 * W_DMA_CHUNKS,)), pltpu.SemaphoreType.DMA((2,)),
                            pltpu.SemaphoreType.DMA((2,))]),
        out_shape=jax.ShapeDtypeStruct(xs.shape, xs.dtype),
        compiler_params=_params(("arbitrary",), 52),
        name="moe_experts",
    )(first, nblk, xs, w_gu, w_down)


def _combine_kernel(d0_ref, d1_ref, x_ref, g_ref, nw_ref, ys_ref, o_ref, buf, sem, *, tt, final_norm):
    i = pl.program_id(0)
    dests = (d0_ref, d1_ref)

    def copies(tile, slot, t):
        return [_row_copy(ys_ref, dref[tile * tt + t], buf.at[slot, k], t, sem.at[slot])
                for k, dref in enumerate(dests)]

    def start(tile, slot):
        def body(t, c):
            for k, cp in enumerate(copies(tile, slot, t)):
                cp.start(priority=k % DMA_THREADS)
            return c
        lax.fori_loop(0, tt, body, 0, unroll=DMA_UNROLL)

    def wait(tile, slot):
        def body(t, c):
            for cp in copies(tile, slot, t):
                cp.wait()
            return c
        lax.fori_loop(0, tt, body, 0, unroll=DMA_UNROLL)

    slot = i & 1

    @pl.when(i == 0)
    def _():
        start(0, 0)

    @pl.when(i + 1 < pl.num_programs(0))
    def _():
        start(i + 1, 1 - slot)

    wait(i, slot)
    g = g_ref[...]
    out = x_ref[...] + sum(g[:, k:k + 1] * _unpack_bf16(buf[slot, k]) for k in range(MOE_TOPK))
    o_ref[...] = _rms(out, nw_ref[...]) if final_norm else out


def _combine(x2, ys, dest, gates_t, final_w, tt=512):
    t, d = x2.shape
    nw = jnp.ones((1, d), F32) if final_w is None else final_w.reshape(1, d).astype(F32)
    return pl.pallas_call(
        functools.partial(_combine_kernel, tt=tt, final_norm=final_w is not None),
        grid_spec=pltpu.PrefetchScalarGridSpec(
            num_scalar_prefetch=2,
            grid=(t // tt,),
            in_specs=[pl.BlockSpec((tt, d), lambda i, d0, d1: (i, 0)),
                      pl.BlockSpec((tt, MOE_TOPK), lambda i, d0, d1: (i, 0)),
                      pl.BlockSpec((1, d), lambda i, d0, d1: (0, 0)),
                      pl.BlockSpec(memory_space=pl.ANY)],
            out_specs=pl.BlockSpec((tt, d), lambda i, d0, d1: (i, 0)),
            scratch_shapes=[pltpu.VMEM((2, MOE_TOPK, tt) + ys.shape[1:], ys.dtype),
                            pltpu.SemaphoreType.DMA((2,))]),
        out_shape=jax.ShapeDtypeStruct((t, d), F32),
        compiler_params=_params(("arbitrary",), 40),
        name="moe_combine",
    )(dest[0], dest[1], x2, gates_t, nw, ys)


def _hier_moe(x2, layer, nw, w_group, b_group, w_expert, b_expert, w_gu, w_down, final_w):
    t = x2.shape[0]
    n_exp = w_expert.shape[1]
    n_blocks = (t * MOE_TOPK) // MOE_ROWS + n_exp
    h, eidx, gates = _route(x2, nw, w_group, b_group, w_expert, b_expert)
    dest, first, nblk = _meta(eidx, n_exp)
    xs = _dispatch(h, dest, first, nblk, n_blocks * MOE_ROWS)
    ys = _experts(xs, first, nblk, layer, w_gu, w_down)
    return _combine(x2, ys, dest, gates.T, final_w)


def kernel(x, s5_lam_re, s5_lam_im, s5_log_dt, s5_b_re, s5_b_im, s5_c_re, s5_c_im, s5_d, s5_w_glu, ml_w_in, ml_b_gate, ml_g_head, ml_w_out, norm_mix, norm_ffn, moe_w_group, moe_b_group, moe_w_expert, moe_b_expert, moe_w_gu, moe_w_down, norm_final):
    bsz, seq, d = x.shape
    assert norm_mix.shape[0] == 2 and s5_w_glu.shape[0] == 1 and ml_w_in.shape[0] == 1
    x2 = x.reshape(bsz * seq, d).astype(F32)

    ops = _s5_operators(s5_lam_re[0], s5_lam_im[0], s5_log_dt[0], s5_b_re[0], s5_b_im[0],
                        s5_c_re[0], s5_c_im[0], seq // S5_CHUNK)
    g = _s5_mix(x2, _row_scale(x2), norm_mix[0].astype(F32), s5_d[0].astype(F32), ops, bsz, seq)
    x2 = _glu(x2, g, s5_w_glu[0].astype(BF16))
    x2 = _hier_moe(x2, 0, norm_ffn[0], moe_w_group[0], moe_b_group[0], moe_w_expert[0], moe_b_expert[0],
                   moe_w_gu, moe_w_down, None)

    n_heads = ml_b_gate.shape[1] // 2
    dv = ml_w_out.shape[1] // n_heads
    dk = (ml_w_in.shape[2] - 2 * n_heads * dv - 2 * n_heads) // (2 * n_heads)
    n_main = 2 * n_heads * (dk + dv)
    z, gcol = _inproj(x2, norm_mix[1], ml_w_in[0, :, :n_main].astype(BF16), ml_w_in[0, :, n_main:], ml_b_gate[0])
    a_bf = _mlstm(z, gcol, gcol.T, ml_g_head[0].astype(F32), bsz, seq, n_heads, dk, dv)
    x2 = _outproj(a_bf, ml_w_out[0].astype(BF16), x2)
    x2 = _hier_moe(x2, 1, norm_ffn[1], moe_w_group[1], moe_b_group[1], moe_w_expert[1], moe_b_expert[1],
                   moe_w_gu, moe_w_down, norm_final)
    return x2.reshape(bsz, seq, d).astype(x.dtype)
```

```python
import functools
import math

import jax
import jax.numpy as jnp
from jax import lax
from jax.experimental import pallas as pl
from jax.experimental.pallas import tpu as pltpu

F32 = jnp.float32
BF16 = jnp.bfloat16
I32 = jnp.int32

EPS = 1e-6
GATE_SOFTCAP = 15.0
MOE_TOPK = 2

LANES = 128
S5_CHUNK = 16
S5_OUT_STEPS = 2
S5_BATCH = 2
ML_CHUNK = 256
ML_HEAD_GROUP = 4
MOE_ROWS_LOG2 = 8
MOE_ROWS = 1 << MOE_ROWS_LOG2
VMEM_BYTES = 64 * 1024 * 1024
DMA_UNROLL = 16
W_DMA_CHUNKS = 4
DMA_THREADS = 2
W_DMA_THREAD = 1
HIGHEST = lax.Precision.HIGHEST


def _params(sem, vmem_mib):
    assert vmem_mib * 1024 * 1024 < VMEM_BYTES
    return pltpu.CompilerParams(dimension_semantics=sem, vmem_limit_bytes=vmem_mib * 1024 * 1024)


def _rms(x32, w):
    ms = jnp.mean(x32 * x32, axis=-1, keepdims=True)
    return x32 * lax.rsqrt(ms + EPS) * w


def _sigmoid(x):
    return 1.0 / (1.0 + jnp.exp(-x))


def _gelu_tanh(x):
    c = math.sqrt(2.0 / math.pi)
    return 0.5 * x * (1.0 + jnp.tanh(c * (x + 0.044715 * (x * x * x))))


def _split_bf16(a32):
    hi = a32.astype(BF16)
    return hi, (a32 - hi.astype(F32)).astype(BF16)


def _dot3(a_hi, a_lo, b_hi, b_lo, dot):
    return (dot(a_hi, b_hi, preferred_element_type=F32) + dot(a_lo, b_hi, preferred_element_type=F32)
            + dot(a_hi, b_lo, preferred_element_type=F32))


def _dot_nt(a, b, preferred_element_type):
    return lax.dot_general(a, b, (((1,), (1,)), ((), ())), preferred_element_type=preferred_element_type)


HI16 = -65536
LO16 = 65535


def _pack_bf16(v):
    half = v.shape[1] // 2
    bits = lax.bitcast_convert_type(v.astype(BF16).astype(F32), I32)
    return (bits[:, half:] & HI16) | ((bits[:, :half] >> 16) & LO16)


def _unpack_bf16(w):
    lo = lax.bitcast_convert_type(w << 16, F32)
    hi = lax.bitcast_convert_type(w & HI16, F32)
    return jnp.concatenate([lo, hi], axis=1)


def _row_scale_kernel(x_ref, r_ref):
    x = x_ref[...]
    r_ref[...] = lax.rsqrt(jnp.mean(x * x, axis=-1, keepdims=True) + EPS)


def _row_scale(x2, tm=512):
    t, d = x2.shape
    return pl.pallas_call(
        _row_scale_kernel,
        grid=(t // tm,),
        in_specs=[pl.BlockSpec((tm, d), lambda i: (i, 0))],
        out_specs=pl.BlockSpec((tm, 1), lambda i: (i, 0)),
        out_shape=jax.ShapeDtypeStruct((t, 1), F32),
        compiler_params=_params(("parallel",), 32),
        name="row_scale",
    )(x2)


def _s5_build_operators(bexp_ref, cexp_ref, qpow_ref, ppow_ref, win_scr, wout_scr, toep_scr):
    L = qpow_ref.shape[1]
    half = bexp_ref.shape[2] // 2

    def tile(ref, s, t):
        return ref.at[s * LANES:(s + 1) * LANES, t * LANES:(t + 1) * LANES]

    br, bi = bexp_ref[0, :, :half], bexp_ref[0, :, half:]
    cr, ci = cexp_ref[0, :half, :], cexp_ref[0, half:, :]
    b_cat = bexp_ref[0].astype(BF16)
    for s in range(L):
        qr, qi = qpow_ref[0, s:s + 1, :half], qpow_ref[0, s:s + 1, half:]
        win_scr[s * LANES:(s + 1) * LANES, :] = jnp.concatenate(
            [br * qr - bi * qi, br * qi + bi * qr], axis=1).astype(BF16)
    for t in range(L + 1):
        pr, pi = ppow_ref[0, :half, t:t + 1], ppow_ref[0, half:, t:t + 1]
        w_t = jnp.concatenate([cr * pr - ci * pi, -(cr * pi + ci * pr)], axis=0).astype(BF16)
        if t >= 1:
            wout_scr[:, (t - 1) * LANES:t * LANES] = w_t
        if t < L:
            tap = jnp.dot(b_cat, w_t, preferred_element_type=F32).astype(BF16)
            for s in range(L - t):
                tile(toep_scr, s, s + t)[...] = tap
    zero = jnp.zeros((LANES, LANES), BF16)
    for s in range(L):
        for tau in range(s):
            tile(toep_scr, s, tau)[...] = zero


def _s5_kernel(x_ref, r_ref, nw_ref, dsk_ref, bexp_ref, cexp_ref, qpow_ref, ppow_ref, lpow_ref, g_ref,
               win_scr, wout_scr, toep_scr, u_scr, *, n_chunks):
    L = qpow_ref.shape[1]
    rows = u_scr.shape[0]

    @pl.when(pl.program_id(1) == 0)
    def _():
        _s5_build_operators(bexp_ref, cexp_ref, qpow_ref, ppow_ref, win_scr, wout_scr, toep_scr)

    def h_step(s):
        return x_ref[pl.ds(s, rows, stride=L), :] * r_ref[:, s:s + 1] * nw_ref[...]

    for s in range(L):
        u_scr[:, s * LANES:(s + 1) * LANES] = h_step(s).astype(BF16)
    v = jnp.dot(u_scr[...], win_scr[...], preferred_element_type=F32)
    half = v.shape[1] // 2
    assert n_chunks & (n_chunks - 1) == 0 and rows % n_chunks == 0
    cidx = lax.broadcasted_iota(I32, (rows, 1), 0) & (n_chunks - 1)

    def shifted(a, d):
        return jnp.where(cidx >= d, pltpu.roll(a, d, 0), 0.0)

    xr = shifted(v[:, :half], 1)
    xi = shifted(v[:, half:], 1)
    d, k = 1, 0
    while d < n_chunks:
        ar = lpow_ref[0, k:k + 1, :half]
        ai = lpow_ref[0, k:k + 1, half:]
        sr = shifted(xr, d)
        si = shifted(xi, d)
        xr, xi = xr + (ar * sr - ai * si), xi + (ar * si + ai * sr)
        d, k = 2 * d, k + 1
    state = jnp.concatenate([xr, xi], axis=1).astype(BF16)
    for j in range(L // S5_OUT_STEPS):
        lo, hi = j * S5_OUT_STEPS * LANES, (j + 1) * S5_OUT_STEPS * LANES
        y = (jnp.dot(u_scr[:, :hi], toep_scr[:hi, lo:hi], preferred_element_type=F32)
             + jnp.dot(state, wout_scr[:, lo:hi], preferred_element_type=F32))
        for k in range(S5_OUT_STEPS):
            tau = j * S5_OUT_STEPS + k
            g_ref[pl.ds(tau, rows, stride=L), :] = _gelu_tanh(
                y[:, k * LANES:(k + 1) * LANES] + dsk_ref[...] * h_step(tau))


def _s5_operators(lam_re, lam_im, log_dt, b_re, b_im, c_re, c_im, n_chunks):
    g, p, cg = b_re.shape
    L = S5_CHUNK
    gt = LANES // cg
    nt = g // gt
    lam = lax.complex(lam_re.astype(F32), lam_im.astype(F32))
    dt = jnp.exp(log_dt.astype(F32))[:, None]
    z = lam * dt
    lam_bar = jnp.exp(z)
    b_bar = ((lam_bar - 1.0) / lam)[..., None] * lax.complex(b_re.astype(F32), b_im.astype(F32))
    cc = lax.complex(c_re.astype(F32), c_im.astype(F32))
    steps = jnp.arange(L + 1, dtype=F32)
    pw = jnp.exp(z[:, None, :] * steps[None, :, None])
    eye = jnp.eye(gt, dtype=F32)

    def re_im(a, axis):
        return jnp.stack([jnp.real(a), jnp.imag(a)], axis=axis)

    def per_tile(a):
        k = a.shape[1]
        return jnp.transpose(a.reshape(nt, gt, k, p), (0, 2, 1, 3)).reshape(nt, k, gt * p)

    bexp = re_im(jnp.transpose(b_bar, (0, 2, 1)), 2).reshape(nt, gt, cg, 2, p)
    bexp = jnp.einsum('iacrp,ab->iacrbp', bexp, eye).reshape(nt, gt * cg, 2 * gt * p)
    cexp = re_im(jnp.transpose(cc, (0, 2, 1)), 1).reshape(nt, gt, 2, p, cg)
    cexp = jnp.einsum('iarpc,ab->irapbc', cexp, eye).reshape(nt, 2 * gt * p, gt * cg)
    q = pw[:, L - 1 - jnp.arange(L), :]
    qpow = jnp.concatenate([per_tile(jnp.real(q)), per_tile(jnp.imag(q))], axis=-1)
    pt = jnp.concatenate([per_tile(jnp.real(pw)), per_tile(jnp.imag(pw))], axis=-1)
    ppow = jnp.zeros((nt, 2 * gt * p, LANES), F32).at[:, :, :L + 1].set(jnp.transpose(pt, (0, 2, 1)))

    n_steps = max(1, (n_chunks - 1).bit_length())
    hops = (2.0 ** jnp.arange(n_steps, dtype=F32)) * L
    lp = jnp.exp(z[:, None, :] * hops[None, :, None])
    lpow = jnp.concatenate([per_tile(jnp.real(lp)), per_tile(jnp.imag(lp))], axis=-1)
    return bexp, cexp, qpow, ppow, lpow


def _s5_mix(x2, r, nw, d_skip, ops, bsz, seq):
    t, d = x2.shape
    L = S5_CHUNK
    n_chunks = seq // L
    nt = d // LANES
    n_state = ops[0].shape[2]
    r2 = r.reshape(t // L, L)
    bb = S5_BATCH if bsz % S5_BATCH == 0 else 1
    rows = bb * n_chunks
    return pl.pallas_call(
        functools.partial(_s5_kernel, n_chunks=n_chunks),
        grid=(nt, bsz // bb),
        in_specs=[pl.BlockSpec((bb * seq, LANES), lambda i, b: (b, i)),
                  pl.BlockSpec((rows, L), lambda i, b: (b, 0)),
                  pl.BlockSpec((1, LANES), lambda i, b: (0, i)),
                  pl.BlockSpec((1, LANES), lambda i, b: (0, i))]
                 + [pl.BlockSpec((1,) + a.shape[1:], lambda i, b: (i, 0, 0)) for a in ops],
        out_specs=pl.BlockSpec((bb * seq, LANES), lambda i, b: (b, i)),
        out_shape=jax.ShapeDtypeStruct((t, d), F32),
        scratch_shapes=[pltpu.VMEM((L * LANES, n_state), BF16), pltpu.VMEM((n_state, L * LANES), BF16),
                        pltpu.VMEM((L * LANES, L * LANES), BF16), pltpu.VMEM((rows, L * LANES), BF16)],
        compiler_params=_params(("arbitrary", "arbitrary"), 56),
        name="s5_scan",
    )(x2, r2, nw.reshape(1, d), d_skip.reshape(1, d), *ops)


def _glu_kernel(g_ref, wv_ref, wg_ref, xres_ref, o_ref, g_scr):
    @pl.when(pl.program_id(1) == 0)
    def _():
        g_scr[...] = g_ref[...].astype(BF16)

    g = g_scr[...]
    val = jnp.dot(g, wv_ref[...], preferred_element_type=F32)
    gate = jnp.dot(g, wg_ref[...], preferred_element_type=F32)
    o_ref[...] = xres_ref[...] + val * _sigmoid(gate)


def _glu(x2, g, w_glu_bf, tm=1024, tn=512):
    t, d = x2.shape
    nj = d // tn
    return pl.pallas_call(
        _glu_kernel,
        grid=(t // tm, nj),
        in_specs=[pl.BlockSpec((tm, d), lambda i, j: (i, 0)),
                  pl.BlockSpec((d, tn), lambda i, j: (0, j)),
                  pl.BlockSpec((d, tn), lambda i, j: (0, j + nj)),
                  pl.BlockSpec((tm, tn), lambda i, j: (i, j))],
        out_specs=pl.BlockSpec((tm, tn), lambda i, j: (i, j)),
        out_shape=jax.ShapeDtypeStruct((t, d), F32),
        scratch_shapes=[pltpu.VMEM((tm, d), BF16)],
        compiler_params=_params(("parallel", "arbitrary"), 48),
        name="s5_glu",
    )(g, w_glu_bf, w_glu_bf, x2)


def _inproj_kernel(x_ref, nw_ref, w_ref, wg_hi_ref, wg_lo_ref, bg_ref, z_ref, g_ref, h_scr, *, n_heads):
    @pl.when(pl.program_id(1) == 0)
    def _():
        h_hi, h_lo = _split_bf16(_rms(x_ref[...], nw_ref[...]))
        h_scr[...] = h_hi
        pre = _dot3(h_hi, h_lo, wg_hi_ref[...], wg_lo_ref[...], jnp.dot) + bg_ref[...]
        t = GATE_SOFTCAP * jnp.tanh(pre / GATE_SOFTCAP)
        log_sig = jnp.minimum(t, 0.0) - jnp.log(1.0 + jnp.exp(-jnp.abs(t)))
        g_ref[...] = jnp.where(lax.broadcasted_iota(I32, pre.shape, 1) >= n_heads, log_sig, t)

    z_ref[...] = jnp.dot(h_scr[...], w_ref[...], preferred_element_type=F32).astype(z_ref.dtype)


def _inproj(x2, nw, w_bf, w_gate, b_gate, tm=1024, tn=1536):
    t, d = x2.shape
    n = w_bf.shape[1]
    ng = w_gate.shape[1]
    wg_hi, wg_lo = _split_bf16(w_gate.astype(F32))
    return pl.pallas_call(
        functools.partial(_inproj_kernel, n_heads=ng // 2),
        grid=(t // tm, n // tn),
        in_specs=[pl.BlockSpec((tm, d), lambda i, j: (i, 0)),
                  pl.BlockSpec((1, d), lambda i, j: (0, 0)),
                  pl.BlockSpec((d, tn), lambda i, j: (0, j)),
                  pl.BlockSpec((d, ng), lambda i, j: (0, 0)),
                  pl.BlockSpec((d, ng), lambda i, j: (0, 0)),
                  pl.BlockSpec((1, ng), lambda i, j: (0, 0))],
        out_specs=[pl.BlockSpec((tm, tn), lambda i, j: (i, j)), pl.BlockSpec((tm, ng), lambda i, j: (i, 0))],
        out_shape=[jax.ShapeDtypeStruct((t, n), BF16), jax.ShapeDtypeStruct((t, ng), F32)],
        scratch_shapes=[pltpu.VMEM((tm, d), BF16)],
        compiler_params=_params(("parallel", "arbitrary"), 48),
        name="ml_inproj",
    )(x2, nw.reshape(1, d), w_bf, wg_hi, wg_lo, b_gate.reshape(1, ng).astype(F32))


def _mlstm_head(head, q, k, v, o, gc, gr, g_head, c_ref, m_ref, n_heads, scale):
    lane_h = lax.broadcasted_iota(I32, gc.shape, 1)
    sub_h = lax.broadcasted_iota(I32, gr.shape, 0)
    i_col = jnp.sum(jnp.where(lane_h == head, gc, 0.0), axis=1, keepdims=True)
    f_col = jnp.sum(jnp.where(lane_h == head + n_heads, gc, 0.0), axis=1, keepdims=True)
    i_row = jnp.sum(jnp.where(sub_h == head, gr, 0.0), axis=0, keepdims=True)
    f_row = jnp.sum(jnp.where(sub_h == head + n_heads, gr, 0.0), axis=0, keepdims=True)

    L = gc.shape[0]
    t_idx = lax.broadcasted_iota(I32, (L, L), 0)
    s_idx = lax.broadcasted_iota(I32, (L, L), 1)
    causal = s_idx <= t_idx
    bcum_col = jnp.sum(jnp.where(causal, f_row, 0.0), axis=1, keepdims=True)
    bcum_row = jnp.sum(jnp.where(t_idx <= s_idx, f_col, 0.0), axis=0, keepdims=True)

    m_prev = m_ref[0:1, 0:1]
    a_log = bcum_col + m_prev
    d_log = jnp.where(causal, bcum_col - bcum_row + i_row, -jnp.inf)
    m_t = jnp.maximum(a_log, jnp.max(d_log, axis=1, keepdims=True))
    w_intra = jnp.exp(d_log - m_t)
    w_inter = jnp.exp(a_log - m_t)

    dv = v.shape[1]
    ones_col = (lax.broadcasted_iota(I32, (L, LANES), 1) == 0).astype(F32)
    v_ext = jnp.concatenate([v, ones_col.astype(BF16)], axis=1)
    c_ext = c_ref[...]
    sc = lax.dot_general(q, k, (((1,), (1,)), ((), ())), preferred_element_type=F32) * (w_intra * scale)
    num_den = (w_inter * jnp.dot(q, c_ext.astype(BF16), preferred_element_type=F32)
               + jnp.dot(sc.astype(BF16), v_ext, preferred_element_type=F32))
    den = num_den[:, dv:dv + 1]
    h_out = num_den[:, :dv] / jnp.maximum(jnp.abs(den), jnp.exp(-m_t))

    b_last = bcum_col[L - 1:L, :]
    g_log = b_last - bcum_col + i_col
    m_new = jnp.maximum(b_last + m_prev, jnp.max(g_log, axis=0, keepdims=True))
    decay = jnp.exp(b_last + m_prev - m_new)
    wk = jnp.exp(g_log - m_new) * scale
    wv = jnp.concatenate([(wk * v.astype(F32)).astype(BF16), (wk * ones_col).astype(BF16)], axis=1)
    kv = lax.dot_general(k, wv, (((0,), (0,)), ((), ())), preferred_element_type=F32)
    c_ref[...] = decay * c_ext + kv
    m_ref[...] = jnp.broadcast_to(m_new, m_ref.shape)

    hn = h_out * lax.rsqrt(jnp.mean(h_out * h_out, axis=-1, keepdims=True) + EPS) * g_head
    return _sigmoid(o.astype(F32)) * hn


def _mlstm_kernel(q_ref, k_ref, v_ref, o_ref, gc_ref, gr_ref, gh_ref, out_ref, c_scr, m_scr,
                  *, n_heads, scale):
    @pl.when(pl.program_id(2) == 0)
    def _():
        c_scr[...] = jnp.zeros_like(c_scr)
        m_scr[...] = jnp.zeros_like(m_scr)

    group, dk = c_scr.shape[:2]
    dv = c_scr.shape[2] - LANES
    gc = gc_ref[...]
    gr = gr_ref[...]
    for j in range(group):
        qk_cols = slice(j * dk, (j + 1) * dk)
        v_cols = slice(j * dv, (j + 1) * dv)
        out = _mlstm_head(pl.program_id(1) * group + j, q_ref[:, qk_cols], k_ref[:, qk_cols], v_ref[:, v_cols],
                          o_ref[:, v_cols], gc, gr, gh_ref[:, v_cols], c_scr.at[j], m_scr.at[j], n_heads, scale)
        out_ref[:, v_cols] = out.astype(out_ref.dtype)


def _mlstm(z, gcol, grow, g_head, bsz, seq, n_heads, dk, dv):
    t = bsz * seq
    L = ML_CHUNK
    nc = seq // L
    hp = ML_HEAD_GROUP if n_heads % ML_HEAD_GROUP == 0 else 1
    qk_w = n_heads * dk
    ng = 2 * n_heads
    k_off = qk_w // (hp * dk)
    v_off = (2 * qk_w) // (hp * dv)
    o_off = (2 * qk_w + n_heads * dv) // (hp * dv)
    return pl.pallas_call(
        functools.partial(_mlstm_kernel, n_heads=n_heads, scale=dk ** -0.5),
        grid=(bsz, n_heads // hp, nc),
        in_specs=[pl.BlockSpec((L, hp * dk), lambda b, h, c: (b * nc + c, h)),
                  pl.BlockSpec((L, hp * dk), lambda b, h, c: (b * nc + c, k_off + h)),
                  pl.BlockSpec((L, hp * dv), lambda b, h, c: (b * nc + c, v_off + h)),
                  pl.BlockSpec((L, hp * dv), lambda b, h, c: (b * nc + c, o_off + h)),
                  pl.BlockSpec((L, ng), lambda b, h, c: (b * nc + c, 0)),
                  pl.BlockSpec((ng, L), lambda b, h, c: (0, b * nc + c)),
                  pl.BlockSpec((1, hp * dv), lambda b, h, c: (0, h))],
        out_specs=pl.BlockSpec((L, hp * dv), lambda b, h, c: (b * nc + c, h)),
        out_shape=jax.ShapeDtypeStruct((t, n_heads * dv), BF16),
        scratch_shapes=[pltpu.VMEM((hp, dk, dv + LANES), F32), pltpu.VMEM((hp, 1, LANES), F32)],
        compiler_params=_params(("parallel", "parallel", "arbitrary"), 40),
        name="ml_chunk",
    )(z, z, z, z, gcol, grow, g_head.reshape(1, n_heads * dv))


def _outproj_kernel(a_ref, w_ref, xres_ref, o_ref):
    o_ref[...] = xres_ref[...] + jnp.dot(a_ref[...], w_ref[...], preferred_element_type=F32)


def _outproj(a_bf, w_bf, x2, tm=512, tn=2048):
    t, kdim = a_bf.shape
    d = w_bf.shape[1]
    return pl.pallas_call(
        _outproj_kernel,
        grid=(t // tm, d // tn),
        in_specs=[pl.BlockSpec((tm, kdim), lambda i, j: (i, 0)),
                  pl.BlockSpec((kdim, tn), lambda i, j: (0, j)),
                  pl.BlockSpec((tm, tn), lambda i, j: (i, j))],
        out_specs=pl.BlockSpec((tm, tn), lambda i, j: (i, j)),
        out_shape=jax.ShapeDtypeStruct((t, d), F32),
        compiler_params=_params(("parallel", "arbitrary"), 40),
        name="ml_outproj",
    )(a_bf, w_bf, x2)


def _route_kernel(x_ref, nw_ref, wt_hi_ref, wt_lo_ref, b_ref, h_ref, eidx_ref, gate_ref, *, n_groups, per_group):
    h_hi, h_lo = _split_bf16(_rms(x_ref[...], nw_ref[...]))
    h_ref[...] = _pack_bf16(h_hi)
    logits = _dot3(wt_hi_ref[...], wt_lo_ref[...], h_hi, h_lo, _dot_nt) + b_ref[...]
    n_exp = n_groups * per_group
    lg = logits[0:n_groups, :]
    le = logits[n_groups:n_groups + n_exp, :]
    g_iota = lax.broadcasted_iota(I32, lg.shape, 0)
    g_max = jnp.max(lg, axis=0, keepdims=True)
    g_sel = jnp.min(jnp.where(lg == g_max, g_iota, n_groups), axis=0, keepdims=True)
    p_sel = 1.0 / jnp.sum(jnp.exp(lg - g_max), axis=0, keepdims=True)

    e_iota = lax.broadcasted_iota(I32, le.shape, 0)
    lo = g_sel * per_group
    cand = jnp.where((e_iota >= lo) & (e_iota < lo + per_group), le, -jnp.inf)
    v1 = jnp.max(cand, axis=0, keepdims=True)
    i1 = jnp.min(jnp.where(cand == v1, e_iota, n_exp), axis=0, keepdims=True)
    cand2 = jnp.where(e_iota == i1, -jnp.inf, cand)
    v2 = jnp.max(cand2, axis=0, keepdims=True)
    i2 = jnp.min(jnp.where(cand2 == v2, e_iota, n_exp), axis=0, keepdims=True)
    e2 = jnp.exp(v2 - v1)
    inv = p_sel / (1.0 + e2)
    eidx_ref[...] = jnp.concatenate([i1, i2], axis=0)
    gate_ref[...] = jnp.concatenate([inv, e2 * inv], axis=0)


def _route(x2, nw, w_group, b_group, w_expert, b_expert, tm=1024):
    t, d = x2.shape
    n_groups = w_group.shape[1]
    n_exp = w_expert.shape[1]
    n_rows = -(-(n_groups + n_exp) // LANES) * LANES
    wt = jnp.zeros((n_rows, d), F32).at[:n_groups].set(w_group.T.astype(F32))
    wt = wt.at[n_groups:n_groups + n_exp].set(w_expert.T.astype(F32))
    bias = jnp.zeros((n_rows, 1), F32).at[:n_groups, 0].set(b_group.astype(F32))
    bias = bias.at[n_groups:n_groups + n_exp, 0].set(b_expert.astype(F32))
    return pl.pallas_call(
        functools.partial(_route_kernel, n_groups=n_groups, per_group=n_exp // n_groups),
        grid=(t // tm,),
        in_specs=[pl.BlockSpec((tm, d), lambda i: (i, 0)),
                  pl.BlockSpec((1, d), lambda i: (0, 0)),
                  pl.BlockSpec((n_rows, d), lambda i: (0, 0)),
                  pl.BlockSpec((n_rows, d), lambda i: (0, 0)),
                  pl.BlockSpec((n_rows, 1), lambda i: (0, 0))],
        out_specs=[pl.BlockSpec((tm, d // 2), lambda i: (i, 0)),
                   pl.BlockSpec((MOE_TOPK, tm), lambda i: (0, i)),
                   pl.BlockSpec((MOE_TOPK, tm), lambda i: (0, i))],
        out_shape=[jax.ShapeDtypeStruct((t, d // 2), I32),
                   jax.ShapeDtypeStruct((MOE_TOPK, t), I32),
                   jax.ShapeDtypeStruct((MOE_TOPK, t), F32)],
        compiler_params=_params(("parallel",), 40),
        name="moe_route",
    )(x2, nw.reshape(1, d), *_split_bf16(wt), bias)


def _meta_kernel(eidx_ref, dest_ref, first_ref, nblk_ref, *, n_exp, n_tiles):
    tl = eidx_ref.shape[1]
    e_iota = lax.broadcasted_iota(I32, (n_exp, tl), 0)
    upper = (lax.broadcasted_iota(I32, (tl, tl), 0) < lax.broadcasted_iota(I32, (tl, tl), 1)).astype(BF16)

    def hits(i):
        return [e_iota == eidx_ref[pl.ds(k * n_tiles + i, 1), :] for k in range(MOE_TOPK)]

    def rank_tile(i, carry):
        hit = hits(i)
        member = sum(h.astype(F32) for h in hit)
        before = jnp.dot(member.astype(BF16), upper, preferred_element_type=F32) + carry
        for k in range(MOE_TOPK):
            rank = jnp.sum(jnp.where(hit[k], before, 0.0), axis=0, keepdims=True)
            dest_ref[pl.ds(k * n_tiles + i, 1), :] = rank.astype(I32)
        return carry + jnp.sum(member, axis=1, keepdims=True)

    counts = lax.fori_loop(0, n_tiles, rank_tile, jnp.zeros((n_exp, 1), F32)).astype(I32)
    nblk = (counts + (MOE_ROWS - 1)) >> MOE_ROWS_LOG2
    lower = (lax.broadcasted_iota(I32, (n_exp, n_exp), 1) < lax.broadcasted_iota(I32, (n_exp, n_exp), 0))
    first = jnp.dot(lower.astype(F32), jnp.broadcast_to(nblk.astype(F32), (n_exp, LANES)),
                    preferred_element_type=F32, precision=HIGHEST)[:, 0:1].astype(I32)
    pad_start = first << MOE_ROWS_LOG2

    def dest_tile(i, carry):
        hit = hits(i)
        for k in range(MOE_TOPK):
            base = jnp.sum(jnp.where(hit[k], pad_start, 0), axis=0, keepdims=True)
            row = pl.ds(k * n_tiles + i, 1)
            dest_ref[row, :] = dest_ref[row, :] + base
        return carry

    lax.fori_loop(0, n_tiles, dest_tile, 0)
    first_ref[...] = first
    nblk_ref[...] = nblk


def _meta(eidx, n_exp, tl=512):
    n_tok = eidx.shape[1]
    n_tiles = n_tok // tl
    dest, first, nblk = pl.pallas_call(
        functools.partial(_meta_kernel, n_exp=n_exp, n_tiles=n_tiles),
        out_shape=[jax.ShapeDtypeStruct((MOE_TOPK * n_tiles, tl), I32),
                   jax.ShapeDtypeStruct((n_exp, 1), I32),
                   jax.ShapeDtypeStruct((n_exp, 1), I32)],
        compiler_params=pltpu.CompilerParams(vmem_limit_bytes=32 * 1024 * 1024),
        name="moe_meta",
    )(eidx.reshape(MOE_TOPK * n_tiles, tl))
    return dest.reshape(MOE_TOPK, n_tok), first[:, 0], nblk[:, 0]


def _row_copy(src_ref, src_row, dst_ref, dst_row, sem):
    return pltpu.make_async_copy(src_ref.at[pl.ds(src_row, 1), :], dst_ref.at[pl.ds(dst_row, 1), :], sem)


def _dispatch_kernel(d0_ref, d1_ref, first_ref, nblk_ref, h_ref, xs_ref, zero_scr, sem, *, tt, n_exp):
    base = pl.program_id(0) * tt
    dests = (d0_ref, d1_ref)

    def pad_copy(e):
        last = pl.multiple_of((first_ref[e] + nblk_ref[e] - 1) * MOE_ROWS, MOE_ROWS)
        return pltpu.make_async_copy(zero_scr, xs_ref.at[pl.ds(last, MOE_ROWS), :], sem)

    def tail_copy(b):
        return pltpu.make_async_copy(
            zero_scr, xs_ref.at[pl.ds(pl.multiple_of(b * MOE_ROWS, MOE_ROWS), MOE_ROWS), :], sem)

    @pl.when(pl.program_id(0) == 0)
    def _():
        zero_scr[...] = jnp.zeros_like(zero_scr)
        for e in range(n_exp):
            @pl.when(nblk_ref[e] > 0)
            def _():
                pad_copy(e).start()
        for e in range(n_exp):
            @pl.when(nblk_ref[e] > 0)
            def _():
                pad_copy(e).wait()
        n_used = first_ref[n_exp - 1] + nblk_ref[n_exp - 1]
        n_blocks = xs_ref.shape[0] // MOE_ROWS
        lax.fori_loop(n_used, n_blocks, lambda b, c: (tail_copy(b).start(), c)[1], 0)
        lax.fori_loop(n_used, n_blocks, lambda b, c: (tail_copy(b).wait(), c)[1], 0)

    def start(t, c):
        for k, dref in enumerate(dests):
            _row_copy(h_ref, t, xs_ref, dref[base + t], sem).start(priority=k % DMA_THREADS)
        return c

    def wait(t, c):
        for dref in dests:
            _row_copy(h_ref, t, xs_ref, dref[base + t], sem).wait()
        return c

    lax.fori_loop(0, tt, start, 0, unroll=DMA_UNROLL)
    lax.fori_loop(0, tt, wait, 0, unroll=DMA_UNROLL)


def _dispatch(h, dest, first, nblk, n_slots, tt=2048):
    n_tok, d = h.shape
    tt = min(tt, n_tok)
    return pl.pallas_call(
        functools.partial(_dispatch_kernel, tt=tt, n_exp=first.shape[0]),
        grid_spec=pltpu.PrefetchScalarGridSpec(
            num_scalar_prefetch=4,
            grid=(n_tok // tt,),
            in_specs=[pl.BlockSpec((tt, d), lambda i, *_: (i, 0))],
            out_specs=pl.BlockSpec(memory_space=pl.ANY),
            scratch_shapes=[pltpu.VMEM((MOE_ROWS, d), h.dtype), pltpu.SemaphoreType.DMA(())]),
        out_shape=jax.ShapeDtypeStruct((n_slots, d), h.dtype),
        compiler_params=_params(("arbitrary",), 40),
        name="moe_dispatch",
    )(dest[0], dest[1], first, nblk, h)


def _expert_kernel(first_ref, nblk_ref, xs_ref, wgu_hbm, wdn_hbm, ys_ref, wgu_stage, wdn_stage, wgu_scr, wdn_scr,
                   xbuf, ybuf, sem_w, sem_x, sem_y, *, layer, hidden):
    e = pl.program_id(0)
    n_exp = pl.num_programs(0)
    nb = nblk_ref[e]
    first = first_ref[e]

    chunks = []
    for hbm, stage, scr in ((wgu_hbm, wgu_stage, wgu_scr), (wdn_hbm, wdn_stage, wdn_scr)):
        rows = stage.shape[0] // W_DMA_CHUNKS
        for c in range(W_DMA_CHUNKS):
            chunks.append((hbm, stage, scr, pl.ds(c * rows, rows), len(chunks)))

    def w_copy(ex, chunk):
        hbm, stage, _, sl, idx = chunk
        return pltpu.make_async_copy(hbm.at[layer, ex, sl, :], stage.at[sl, :], sem_w.at[idx])

    def x_copy(b, slot):
        rows = pl.ds(pl.multiple_of((first + b) * MOE_ROWS, MOE_ROWS), MOE_ROWS)
        return pltpu.make_async_copy(xs_ref.at[rows, :], xbuf.at[slot], sem_x.at[slot])

    def y_copy(b, slot):
        rows = pl.ds(pl.multiple_of((first + b) * MOE_ROWS, MOE_ROWS), MOE_ROWS)
        return pltpu.make_async_copy(ybuf.at[slot], ys_ref.at[rows, :], sem_y.at[slot])

    @pl.when(e == 0)
    def _():
        for chunk in chunks:
            w_copy(0, chunk).start(priority=W_DMA_THREAD)

    @pl.when(nb > 0)
    def _():
        x_copy(0, 0).start()

    for chunk in chunks:
        _, stage, scr, sl, _ = chunk
        w_copy(e, chunk).wait()
        scr[sl, :] = stage[sl, :].astype(BF16)

        @pl.when(e + 1 < n_exp)
        def _():
            w_copy(e + 1, chunk).start(priority=W_DMA_THREAD)

    def block(b, carry):
        slot = b & 1
        x_copy(b, slot).wait()

        @pl.when(b + 1 < nb)
        def _():
            x_copy(b + 1, 1 - slot).start()

        @pl.when(b >= 2)
        def _():
            y_copy(b - 2, slot).wait()

        gu = jnp.dot(_unpack_bf16(xbuf[slot]).astype(BF16), wgu_scr[...], preferred_element_type=F32)
        g = gu[:, :hidden]
        act = (g * _sigmoid(g)) * gu[:, hidden:]
        ybuf[slot] = _pack_bf16(jnp.dot(act.astype(BF16), wdn_scr[...], preferred_element_type=F32))
        y_copy(b, slot).start()
        return carry

    lax.fori_loop(0, nb, block, 0)

    @pl.when(nb >= 2)
    def _():
        y_copy(nb - 2, nb & 1).wait()

    @pl.when(nb >= 1)
    def _():
        y_copy(nb - 1, (nb - 1) & 1).wait()

    @pl.when(e == n_exp - 1)
    def _():
        ybuf[0] = jnp.zeros(ybuf.shape[1:], ybuf.dtype)
        n_blocks = ys_ref.shape[0] // MOE_ROWS
        lax.fori_loop(nb, n_blocks - first, lambda b, c: (y_copy(b, 0).start(), c)[1], 0)
        lax.fori_loop(nb, n_blocks - first, lambda b, c: (y_copy(b, 0).wait(), c)[1], 0)


def _experts(xs, first, nblk, layer, w_gu, w_down):
    n_exp, d, two_f = w_gu.shape[1], w_gu.shape[2], w_gu.shape[3]
    hidden = two_f // 2
    return pl.pallas_call(
        functools.partial(_expert_kernel, layer=layer, hidden=hidden),
        grid_spec=pltpu.PrefetchScalarGridSpec(
            num_scalar_prefetch=2,
            grid=(n_exp,),
            in_specs=[pl.BlockSpec(memory_space=pl.ANY),
                      pl.BlockSpec(memory_space=pl.ANY),
                      pl.BlockSpec(memory_space=pl.ANY)],
            out_specs=pl.BlockSpec(memory_space=pl.ANY),
            scratch_shapes=[pltpu.VMEM((d, two_f), w_gu.dtype), pltpu.VMEM((hidden, d), w_down.dtype),
                            pltpu.VMEM((d, two_f), BF16), pltpu.VMEM((hidden, d), BF16),
                            pltpu.VMEM((2, MOE_ROWS) + xs.shape[1:], xs.dtype),
                            pltpu.VMEM((2, MOE_ROWS) + xs.shape[1:], xs.dtype),
                            pltpu.SemaphoreType.DMA((2 * W_DMA_CHUNKS,)), pltpu.SemaphoreType.DMA((2,)),
                            pltpu.SemaphoreType.DMA((2,))]),
        out_shape=jax.ShapeDtypeStruct(xs.shape, xs.dtype),
        compiler_params=_params(("arbitrary",), 52),
        name="moe_experts",
    )(first, nblk, xs, w_gu, w_down)


def _combine_kernel(d0_ref, d1_ref, x_ref, g_ref, nw_ref, ys_ref, o_ref, buf, sem, *, tt, final_norm):
    i = pl.program_id(0)
    dests = (d0_ref, d1_ref)

    def copies(tile, slot, t):
        return [_row_copy(ys_ref, dref[tile * tt + t], buf.at[slot, k], t, sem.at[slot])
                for k, dref in enumerate(dests)]

    def start(tile, slot):
        def body(t, c):
            for k, cp in enumerate(copies(tile, slot, t)):
                cp.start(priority=k % DMA_THREADS)
            return c
        lax.fori_loop(0, tt, body, 0, unroll=DMA_UNROLL)

    def wait(tile, slot):
        def body(t, c):
            for cp in copies(tile, slot, t):
                cp.wait()
            return c
        lax.fori_loop(0, tt, body, 0, unroll=DMA_UNROLL)

    slot = i & 1

    @pl.when(i == 0)
    def _():
        start(0, 0)

    @pl.when(i + 1 < pl.num_programs(0))
    def _():
        start(i + 1, 1 - slot)

    wait(i, slot)
    g = g_ref[...]
    out = x_ref[...] + sum(g[:, k:k + 1] * _unpack_bf16(buf[slot, k]) for k in range(MOE_TOPK))
    o_ref[...] = _rms(out, nw_ref[...]) if final_norm else out


def _combine(x2, ys, dest, gates_t, final_w, tt=512):
    t, d = x2.shape
    nw = jnp.ones((1, d), F32) if final_w is None else final_w.reshape(1, d).astype(F32)
    return pl.pallas_call(
        functools.partial(_combine_kernel, tt=tt, final_norm=final_w is not None),
        grid_spec=pltpu.PrefetchScalarGridSpec(
            num_scalar_prefetch=2,
            grid=(t // tt,),
            in_specs=[pl.BlockSpec((tt, d), lambda i, d0, d1: (i, 0)),
                      pl.BlockSpec((tt, MOE_TOPK), lambda i, d0, d1: (i, 0)),
                      pl.BlockSpec((1, d), lambda i, d0, d1: (0, 0)),
                      pl.BlockSpec(memory_space=pl.ANY)],
            out_specs=pl.BlockSpec((tt, d), lambda i, d0, d1: (i, 0)),
            scratch_shapes=[pltpu.VMEM((2, MOE_TOPK, tt) + ys.shape[1:], ys.dtype),
                            pltpu.SemaphoreType.DMA((2,))]),
        out_shape=jax.ShapeDtypeStruct((t, d), F32),
        compiler_params=_params(("arbitrary",), 40),
        name="moe_combine",
    )(dest[0], dest[1], x2, gates_t, nw, ys)


def _hier_moe(x2, layer, nw, w_group, b_group, w_expert, b_expert, w_gu, w_down, final_w):
    t = x2.shape[0]
    n_exp = w_expert.shape[1]
    n_blocks = (t * MOE_TOPK) // MOE_ROWS + n_exp
    h, eidx, gates = _route(x2, nw, w_group, b_group, w_expert, b_expert)
    dest, first, nblk = _meta(eidx, n_exp)
    xs = _dispatch(h, dest, first, nblk, n_blocks * MOE_ROWS)
    ys = _experts(xs, first, nblk, layer, w_gu, w_down)
    return _combine(x2, ys, dest, gates.T, final_w)


def kernel(x, s5_lam_re, s5_lam_im, s5_log_dt, s5_b_re, s5_b_im, s5_c_re, s5_c_im, s5_d, s5_w_glu, ml_w_in, ml_b_gate, ml_g_head, ml_w_out, norm_mix, norm_ffn, moe_w_group, moe_b_group, moe_w_expert, moe_b_expert, moe_w_gu, moe_w_down, norm_final):
    bsz, seq, d = x.shape
    assert norm_mix.shape[0] == 2 and s5_w_glu.shape[0] == 1 and ml_w_in.shape[0] == 1
    x2 = x.reshape(bsz * seq, d).astype(F32)

    ops = _s5_operators(s5_lam_re[0], s5_lam_im[0], s5_log_dt[0], s5_b_re[0], s5_b_im[0],
                        s5_c_re[0], s5_c_im[0], seq // S5_CHUNK)
    g = _s5_mix(x2, _row_scale(x2), norm_mix[0].astype(F32), s5_d[0].astype(F32), ops, bsz, seq)
    x2 = _glu(x2, g, s5_w_glu[0].astype(BF16))
    x2 = _hier_moe(x2, 0, norm_ffn[0], moe_w_group[0], moe_b_group[0], moe_w_expert[0], moe_b_expert[0],
                   moe_w_gu, moe_w_down, None)

    n_heads = ml_b_gate.shape[1] // 2
    dv = ml_w_out.shape[1] // n_heads
    dk = (ml_w_in.shape[2] - 2 * n_heads * dv - 2 * n_heads) // (2 * n_heads)
    n_main = 2 * n_heads * (dk + dv)
    z, gcol = _inproj(x2, norm_mix[1], ml_w_in[0, :, :n_main].astype(BF16), ml_w_in[0, :, n_main:], ml_b_gate[0])
    a_bf = _mlstm(z, gcol, gcol.T, ml_g_head[0].astype(F32), bsz, seq, n_heads, dk, dv)
    x2 = _outproj(a_bf, ml_w_out[0].astype(BF16), x2)
    x2 = _hier_moe(x2, 1, norm_ffn[1], moe_w_group[1], moe_b_group[1], moe_w_expert[1], moe_b_expert[1],
                   moe_w_gu, moe_w_down, norm_final)
    return x2.reshape(bsz, seq, d).astype(x.dtype)
```
